```python
import math
import jax, jax.numpy as jnp
from jax import lax
import numpy as np

D_MODEL = 1024
BATCH = 2
SEQ = 8192
DEPTH = 2
DEC_BATCH = 128
DEC_SEQ = 1
PAST_LEN = 16384
PAGE_SIZE = 128

N_MIX_GROUPS = 4
GROUP_W = D_MODEL // N_MIX_GROUPS
LRU_W = GROUP_W
LRU_BLOCKS = 4
LRU_BW = LRU_W // LRU_BLOCKS
CONV_W = 4
LRU_C = 8.0
RET_HEADS = 4
RET_DV = GROUP_W // RET_HEADS
RET_DK = RET_DV // 2
RET_CHUNK = 128
MLA_HEADS = 4
MLA_V = GROUP_W // MLA_HEADS
MLA_NOPE = MLA_V
MLA_ROPE = MLA_V // 2
MLA_Q_RANK = (D_MODEL * 3) // 16
MLA_KV_RANK = D_MODEL // 8
MLA_SCALE = (MLA_NOPE + MLA_ROPE) ** -0.5
DIFF_HEADS = 4
DIFF_KV_HEADS = 2
DIFF_REP = DIFF_HEADS // DIFF_KV_HEADS
DIFF_V = GROUP_W // DIFF_HEADS
DIFF_D = DIFF_V // 2
DIFF_SCALE = DIFF_D ** -0.5
MOE_GROUPS = 4
MOE_PER_GROUP = 4
MOE_EXPERTS = MOE_GROUPS * MOE_PER_GROUP
MOE_TOPK = 2
MOE_FF = D_MODEL // 4

Q_BLOCK = 128
ROPE_THETA = 10000.0
EPS = 1e-6
F32 = jnp.float32
IN_SIZES = (LRU_W, LRU_W, RET_HEADS * RET_DK, RET_HEADS * RET_DK, RET_HEADS * RET_DV, GROUP_W,
            MLA_Q_RANK, MLA_KV_RANK, MLA_ROPE,
            DIFF_HEADS * 2 * DIFF_D, DIFF_KV_HEADS * 2 * DIFF_D, DIFF_KV_HEADS * DIFF_V)

kernel_name = 'hymba_style_hybrid_decode_step'


def rmsnorm(x, g):
    xf = x.astype(F32)
    y = xf * lax.rsqrt(jnp.mean(xf * xf, axis=-1, keepdims=True) + EPS)
    return (y * g.astype(F32)).astype(x.dtype)


def rope(x, pos):
    half = x.shape[-1] // 2
    inv = ROPE_THETA ** (-jnp.arange(half, dtype=F32) / half)
    ang = pos.astype(F32)[:, None] * inv[None, :]
    ang = ang.reshape((1, pos.shape[0]) + (1,) * (x.ndim - 3) + (half,))
    c, s = jnp.cos(ang), jnp.sin(ang)
    xf = x.astype(F32)
    x1, x2 = xf[..., :half], xf[..., half:]
    return jnp.concatenate([x1 * c - x2 * s, x1 * s + x2 * c], axis=-1).astype(x.dtype)


def split_in(z):
    points = [int(p) for p in np.cumsum(IN_SIZES)[:-1]]
    return jnp.split(z, points, axis=-1)


def _lin_combine(c1, c2):
    a1, b1 = c1
    a2, b2 = c2
    return a1 * a2, a2 * b1 + b2


def lru_mixer(a_x, a_g, conv_buf, h0, l, P):
    N, T, _ = a_x.shape
    xf = jnp.concatenate([conv_buf.astype(a_x.dtype), a_x], axis=1)
    w = P['conv_w'][l]
    xc = P['conv_b'][l] + sum(xf[:, k:k + T] * w[k] for k in range(CONV_W))
    new_buf = xf[:, T:]
    xb = xc.reshape(N, T, LRU_BLOCKS, LRU_BW)
    r = jax.nn.sigmoid(jnp.einsum('ntbi,bij->ntbj', xb, P['lru_wa'][l]).reshape(N, T, LRU_W) + P['lru_ba'][l])
    i = jax.nn.sigmoid(jnp.einsum('ntbi,bij->ntbj', xb, P['lru_wx'][l]).reshape(N, T, LRU_W) + P['lru_bx'][l])
    log_a = -LRU_C * r.astype(F32) * jax.nn.softplus(-P['lru_lambda'][l].astype(F32))
    a = jnp.exp(log_a)
    b = jnp.sqrt(-jnp.expm1(2.0 * log_a)) * (i * xc).astype(F32)
    a_cum, h_zero = lax.associative_scan(_lin_combine, (a, b), axis=1)
    hs = h_zero + a_cum * h0.astype(F32)[:, None, :]
    out = hs.astype(a_x.dtype) * jax.nn.gelu(a_g)
    return out, new_buf, hs[:, -1].astype(h0.dtype)


def retention_chunk(S0, q, k, v, logg):
    L = q.shape[1]
    idx = jnp.arange(L, dtype=F32)
    diff = idx[:, None] - idx[None, :]
    D = jnp.where(diff >= 0, jnp.exp(jnp.maximum(diff, 0.0)[None] * logg[:, None, None]), 0.0)
    s = jnp.einsum('nihd,njhd->nhij', q, k) * D
    o = jnp.einsum('nhij,njhe->nihe', s, v)
    o = o + jnp.einsum('nihd,nhde->nihe', q, S0) * jnp.exp((idx + 1.0)[:, None] * logg[None, :])[None, :, :, None]
    kd = k * jnp.exp((L - 1.0 - idx)[:, None] * logg[None, :])[None, :, :, None]
    S = jnp.exp(L * logg)[None, :, None, None] * S0 + jnp.einsum('njhd,njhe->nhde', kd, v)
    return S, o


def retention_mixer(q, k, v, g, S0, l, P):
    N, T = q.shape[:2]
    logg = jnp.log1p(-jnp.exp2(-5.0 - jnp.arange(RET_HEADS, dtype=F32)))
    C = RET_CHUNK if T % RET_CHUNK == 0 else T
    nC = T // C
    def chunks(a):
        return a.astype(F32).reshape((N, nC, C) + a.shape[2:]).swapaxes(0, 1)
    S, o = lax.scan(lambda S, xs: retention_chunk(S, xs[0], xs[1], xs[2], logg), S0.astype(F32),
                    (chunks(q), chunks(k), chunks(v)))
    o = o.swapaxes(0, 1).reshape(N, T, RET_HEADS, RET_DV)
    mu = jnp.mean(o, axis=-1, keepdims=True)
    var = jnp.mean(jnp.square(o - mu), axis=-1, keepdims=True)
    on = ((o - mu) * lax.rsqrt(var + EPS)).reshape(N, T, GROUP_W) * P['ret_gn'][l].astype(F32)
    out = jax.nn.silu(g) * on.astype(g.dtype)
    return out, S.astype(S0.dtype)


def mla_scores(q_lat, q_pe, ckv, kpe):
    return (jnp.einsum('nthc,nsc->nhts', q_lat, ckv, preferred_element_type=F32)
            + jnp.einsum('nthr,nsr->nhts', q_pe, kpe, preferred_element_type=F32)) * MLA_SCALE


def diff_scores(q, k):
    return jnp.einsum('ntgrmd,nsgmd->ngrmts', q, k, preferred_element_type=F32) * DIFF_SCALE


def mla_prompt(l, q_lat, q_pe, ckv, kpe):
    B, S, H, C = q_lat.shape
    nb = S // Q_BLOCK
    kpos = jnp.arange(S)
    ckv_f = ckv.astype(F32)
    def block(args):
        i, ql, qp = args
        s = mla_scores(ql, qp, ckv, kpe)
        qpos = i * Q_BLOCK + jnp.arange(Q_BLOCK)
        s = jnp.where(kpos[None, :] <= qpos[:, None], s, -jnp.inf)
        p = jax.nn.softmax(s, axis=-1)
        return jnp.einsum('bhts,bsc->bthc', p, ckv_f)
    qb = q_lat.reshape(B, nb, Q_BLOCK, H, C).swapaxes(0, 1)
    pb = q_pe.reshape(B, nb, Q_BLOCK, H, MLA_ROPE).swapaxes(0, 1)
    o = lax.map(block, (jnp.arange(nb), qb, pb))
    return o.swapaxes(0, 1).reshape(B, S, H, C).astype(q_lat.dtype)


def diff_prompt(l, q, k, v):
    B, S = q.shape[:2]
    nb = S // Q_BLOCK
    kpos = jnp.arange(S)
    vf = v.astype(F32)
    def block(args):
        i, qi = args
        s = diff_scores(qi, k)
        qpos = i * Q_BLOCK + jnp.arange(Q_BLOCK)
        s = jnp.where(kpos[None, :] <= qpos[:, None], s, -jnp.inf)
        p = jax.nn.softmax(s, axis=-1)
        return jnp.einsum('bgrmts,bsge->btgrme', p, vf)
    qb = q.reshape((B, nb, Q_BLOCK) + q.shape[2:]).swapaxes(0, 1)
    o = lax.map(block, (jnp.arange(nb), qb))
    return o.swapaxes(0, 1).reshape((B, S) + o.shape[3:])


def online_attend(s0, v0, page_table, page_fn, pv):
    s0 = s0.astype(F32)
    m = jnp.max(s0, axis=-1)
    p = jnp.exp(s0 - m[..., None])
    den = jnp.sum(p, axis=-1)
    acc = pv(p, v0)
    def body(carry, phys):
        m, den, acc = carry
        s, vals = page_fn(phys)
        m_new = jnp.maximum(m, jnp.max(s, axis=-1))
        alpha = jnp.exp(m - m_new)
        p = jnp.exp(s - m_new[..., None])
        return (m_new, alpha * den + jnp.sum(p, axis=-1), alpha[..., None] * acc + pv(p, vals)), None
    (m, den, acc), _ = lax.scan(body, (m, den, acc), page_table.T)
    return acc / den[..., None]


def hier_moe(h, l, P):
    g_logit = jnp.dot(h, P['moe_w_rg'][l], preferred_element_type=F32) + P['moe_b_rg'][l].astype(F32)
    g_prob = jax.nn.softmax(g_logit, axis=-1)
    g_sel = jnp.argmax(g_logit, axis=-1)
    g_w = jnp.max(g_prob, axis=-1)
    e_logit = (jnp.dot(h, P['moe_w_re'][l], preferred_element_type=F32)
               + P['moe_b_re'][l].astype(F32)).reshape(-1, MOE_GROUPS, MOE_PER_GROUP)
    e_sel = jnp.einsum('mge,mg->me', e_logit, jax.nn.one_hot(g_sel, MOE_GROUPS, dtype=F32))
    top_v, top_i = lax.top_k(e_sel, MOE_TOPK)
    w2 = jax.nn.softmax(top_v, axis=-1) * g_w[:, None]
    ids = g_sel[:, None] * MOE_PER_GROUP + top_i
    combine = jnp.sum(jax.nn.one_hot(ids, MOE_EXPERTS, dtype=F32) * w2[..., None], axis=1)
    hg = jnp.einsum('md,xdf->mxf', h, P['moe_w_gate'][l])
    hu = jnp.einsum('md,xdf->mxf', h, P['moe_w_up'][l])
    act = jax.nn.silu(hg) * hu * combine[..., None].astype(h.dtype)
    return jnp.einsum('mxf,xfd->md', act, P['moe_w_down'][l])


def trunk_layer(l, x, pos, lru_buf, lru_h, ret_S, attend_mla, attend_diff, P):
    N, T, _ = x.shape
    h = rmsnorm(x, P['norm1'][l])
    z = h @ P['w_in'][l]
    a_x, a_g, b_q, b_k, b_v, b_g, c_q, c_kv, c_pe, d_q, d_k, d_v = split_in(z)
    y_a, lru_buf, lru_h = lru_mixer(a_x, a_g, lru_buf, lru_h, l, P)
    rq = rope(b_q.reshape(N, T, RET_HEADS, RET_DK), pos)
    rk = rope(b_k.reshape(N, T, RET_HEADS, RET_DK), pos) * (RET_DK ** -0.5)
    rv = b_v.reshape(N, T, RET_HEADS, RET_DV)
    y_b, ret_S = retention_mixer(rq, rk, rv, b_g, ret_S, l, P)
    cq = rmsnorm(c_q, P['mla_q_norm'][l])
    q = (cq @ P['mla_w_uq'][l]).reshape(N, T, MLA_HEADS, MLA_NOPE + MLA_ROPE)
    q_pe = rope(q[..., MLA_NOPE:], pos)
    q_lat = jnp.einsum('nthd,chd->nthc', q[..., :MLA_NOPE], P['mla_w_uk'][l])
    ckv = rmsnorm(c_kv, P['mla_kv_norm'][l])
    kpe = rope(c_pe, pos)
    o_lat = attend_mla(l, q_lat, q_pe, ckv, kpe)
    y_c = jnp.einsum('nthc,chd->nthd', o_lat, P['mla_w_uv'][l]).reshape(N, T, GROUP_W)
    dq = d_q.reshape(N, T, DIFF_KV_HEADS, DIFF_REP, 2, DIFF_D)
    dk = d_k.reshape(N, T, DIFF_KV_HEADS, 2, DIFF_D)
    dv = d_v.reshape(N, T, DIFF_KV_HEADS, DIFF_V)
    o_maps = attend_diff(l, dq, dk, dv)
    lam_init = 0.8 - 0.6 * math.exp(-0.3 * l)
    lam = (jnp.exp(jnp.sum(P['diff_lq1'][l].astype(F32) * P['diff_lk1'][l].astype(F32)))
           - jnp.exp(jnp.sum(P['diff_lq2'][l].astype(F32) * P['diff_lk2'][l].astype(F32))) + lam_init)
    a = o_maps[..., 0, :] - lam * o_maps[..., 1, :]
    y_d = (rmsnorm(a, P['diff_subln'][l]) * (1.0 - lam_init)).reshape(N, T, GROUP_W).astype(x.dtype)
    x = x + jnp.concatenate([y_a, y_b, y_c, y_d], axis=-1) @ P['w_out'][l]
    hm = rmsnorm(x, P['norm2'][l])
    x = x + hier_moe(hm.reshape(N * T, D_MODEL), l, P).reshape(N, T, D_MODEL)
    new = (lru_buf, lru_h, ret_S, ckv, kpe, d_k.reshape(N, T, DIFF_KV_HEADS, 2 * DIFF_D), dv)
    return x, new


def setup_inputs(seed: int = 0) -> dict:
    key = jax.random.key(seed)
    ks = iter(jax.random.split(key, 64))
    def nrm(shape, scale=1.0):
        return jax.random.normal(next(ks), shape, F32) * scale
    def gain(shape):
        return 1.0 + 0.01 * jax.random.normal(next(ks), shape, F32)
    n_pages = PAST_LEN // PAGE_SIZE
    n_pool = (DEC_BATCH * n_pages * 5) // 4
    in_width = sum(IN_SIZES)
    page_table = jax.random.permutation(next(ks), n_pool)[:DEC_BATCH * n_pages].reshape(DEC_BATCH, n_pages).astype(jnp.int32)
    u = jax.random.uniform(next(ks), (DEPTH, LRU_W), F32, 0.9, 0.999)
    sa = u ** (1.0 / LRU_C)
    lru_lambda = jnp.log(sa) - jnp.log1p(-sa)
    return {
        'x_prompt': nrm((BATCH, SEQ, D_MODEL)),
        'x_sample': nrm((DEC_BATCH, DEC_SEQ, D_MODEL)),
        'state_lru_conv': nrm((DEPTH, DEC_BATCH, CONV_W - 1, LRU_W)),
        'state_lru_h': nrm((DEPTH, DEC_BATCH, LRU_W), 0.5),
        'state_ret': nrm((DEPTH, DEC_BATCH, RET_HEADS, RET_DK, RET_DV), 0.5),
        'cache_mla_ckv': nrm((DEPTH, n_pool, PAGE_SIZE, MLA_KV_RANK)),
        'cache_mla_kpe': nrm((DEPTH, n_pool, PAGE_SIZE, MLA_ROPE)),
        'cache_diff_k': nrm((DEPTH, n_pool, PAGE_SIZE, DIFF_KV_HEADS, 2 * DIFF_D)),
        'cache_diff_v': nrm((DEPTH, n_pool, PAGE_SIZE, DIFF_KV_HEADS, DIFF_V)),
        'page_table': page_table,
        'norm1': gain((DEPTH, D_MODEL)),
        'w_in': nrm((DEPTH, D_MODEL, in_width), D_MODEL ** -0.5),
        'conv_w': nrm((DEPTH, CONV_W, LRU_W), CONV_W ** -0.5),
        'conv_b': nrm((DEPTH, LRU_W), 0.01),
        'lru_wa': nrm((DEPTH, LRU_BLOCKS, LRU_BW, LRU_BW), LRU_BW ** -0.5),
        'lru_ba': nrm((DEPTH, LRU_W), 0.01),
        'lru_wx': nrm((DEPTH, LRU_BLOCKS, LRU_BW, LRU_BW), LRU_BW ** -0.5),
        'lru_bx': nrm((DEPTH, LRU_W), 0.01),
        'lru_lambda': lru_lambda,
        'ret_gn': gain((DEPTH, GROUP_W)),
        'mla_q_norm': gain((DEPTH, MLA_Q_RANK)),
        'mla_w_uq': nrm((DEPTH, MLA_Q_RANK, MLA_HEADS * (MLA_NOPE + MLA_ROPE)), MLA_Q_RANK ** -0.5),
        'mla_kv_norm': gain((DEPTH, MLA_KV_RANK)),
        'mla_w_uk': nrm((DEPTH, MLA_KV_RANK, MLA_HEADS, MLA_NOPE), MLA_KV_RANK ** -0.5),
        'mla_w_uv': nrm((DEPTH, MLA_KV_RANK, MLA_HEADS, MLA_V), MLA_KV_RANK ** -0.5),
        'diff_lq1': nrm((DEPTH, DIFF_D), 0.1),
        'diff_lk1': nrm((DEPTH, DIFF_D), 0.1),
        'diff_lq2': nrm((DEPTH, DIFF_D), 0.1),
        'diff_lk2': nrm((DEPTH, DIFF_D), 0.1),
        'diff_subln': gain((DEPTH, DIFF_V)),
        'w_out': nrm((DEPTH, D_MODEL, D_MODEL), D_MODEL ** -0.5),
        'norm2': gain((DEPTH, D_MODEL)),
        'moe_w_rg': nrm((DEPTH, D_MODEL, MOE_GROUPS), D_MODEL ** -0.5),
        'moe_b_rg': nrm((DEPTH, MOE_GROUPS), 0.01),
        'moe_w_re': nrm((DEPTH, D_MODEL, MOE_EXPERTS), D_MODEL ** -0.5),
        'moe_b_re': nrm((DEPTH, MOE_EXPERTS), 0.01),
        'moe_w_gate': nrm((DEPTH, MOE_EXPERTS, D_MODEL, MOE_FF), D_MODEL ** -0.5),
        'moe_w_up': nrm((DEPTH, MOE_EXPERTS, D_MODEL, MOE_FF), D_MODEL ** -0.5),
        'moe_w_down': nrm((DEPTH, MOE_EXPERTS, MOE_FF, D_MODEL), MOE_FF ** -0.5),
        'final_norm': gain((D_MODEL,)),
    }


def reference(x_prompt, x_sample, state_lru_conv, state_lru_h, state_ret, cache_mla_ckv, cache_mla_kpe,
              cache_diff_k, cache_diff_v, page_table, norm1, w_in, conv_w, conv_b, lru_wa, lru_ba, lru_wx,
              lru_bx, lru_lambda, ret_gn, mla_q_norm, mla_w_uq, mla_kv_norm, mla_w_uk, mla_w_uv,
              diff_lq1, diff_lk1, diff_lq2, diff_lk2, diff_subln, w_out, norm2, moe_w_rg, moe_b_rg,
              moe_w_re, moe_b_re, moe_w_gate, moe_w_up, moe_w_down, final_norm):
    P = dict(norm1=norm1, w_in=w_in, conv_w=conv_w, conv_b=conv_b, lru_wa=lru_wa, lru_ba=lru_ba,
             lru_wx=lru_wx, lru_bx=lru_bx, lru_lambda=lru_lambda, ret_gn=ret_gn, mla_q_norm=mla_q_norm,
             mla_w_uq=mla_w_uq, mla_kv_norm=mla_kv_norm, mla_w_uk=mla_w_uk, mla_w_uv=mla_w_uv,
             diff_lq1=diff_lq1, diff_lk1=diff_lk1, diff_lq2=diff_lq2, diff_lk2=diff_lk2,
             diff_subln=diff_subln, w_out=w_out, norm2=norm2, moe_w_rg=moe_w_rg, moe_b_rg=moe_b_rg,
             moe_w_re=moe_w_re, moe_b_re=moe_b_re, moe_w_gate=moe_w_gate, moe_w_up=moe_w_up,
             moe_w_down=moe_w_down)
    B, S, _ = x_prompt.shape
    N, T, _ = x_sample.shape
    past_len = page_table.shape[1] * PAGE_SIZE
    pos_p = jnp.arange(S, dtype=jnp.int32)
    pos_s = past_len + jnp.arange(T, dtype=jnp.int32)
    causal = jnp.arange(T)[None, :] <= jnp.arange(T)[:, None]

    def mla_sample(l, q_lat, q_pe, ckv, kpe):
        s0 = jnp.where(causal, mla_scores(q_lat, q_pe, ckv, kpe), -jnp.inf)
        def page_fn(phys):
            c = cache_mla_ckv[l, phys]
            r = cache_mla_kpe[l, phys]
            return mla_scores(q_lat, q_pe, c, r), c
        pv = lambda p, c: jnp.einsum('nhts,nsc->nhtc', p, c.astype(F32))
        o = online_attend(s0, ckv, page_table, page_fn, pv)
        return o.transpose(0, 2, 1, 3).astype(q_lat.dtype)

    def diff_sample(l, q, k, v):
        s0 = jnp.where(causal, diff_scores(q, k), -jnp.inf)
        def page_fn(phys):
            kp = cache_diff_k[l, phys].reshape(N, PAGE_SIZE, DIFF_KV_HEADS, 2, DIFF_D)
            vp = cache_diff_v[l, phys]
            return diff_scores(q, kp), vp
        pv = lambda p, vals: jnp.einsum('ngrmts,nsge->ngrmte', p, vals.astype(F32))
        o = online_attend(s0, v, page_table, page_fn, pv)
        return jnp.moveaxis(o, 4, 1)

    xp, xs = x_prompt, x_sample
    p_new, s_new = [], []
    for l in range(DEPTH):
        xp, st_p = trunk_layer(l, xp, pos_p,
                               jnp.zeros((B, CONV_W - 1, LRU_W), x_prompt.dtype),
                               jnp.zeros((B, LRU_W), x_prompt.dtype),
                               jnp.zeros((B, RET_HEADS, RET_DK, RET_DV), x_prompt.dtype),
                               mla_prompt, diff_prompt, P)
        xs, st_s = trunk_layer(l, xs, pos_s, state_lru_conv[l], state_lru_h[l], state_ret[l],
                               mla_sample, diff_sample, P)
        p_new.append(st_p)
        s_new.append(st_s)
    p_lru_conv, p_lru_h, p_ret, p_ckv, p_kpe, p_dk, p_dv = [jnp.stack([st[i] for st in p_new]) for i in range(7)]
    s_lru_conv, s_lru_h, s_ret, s_ckv, s_kpe, s_dk, s_dv = [jnp.stack([st[i] for st in s_new]) for i in range(7)]
    y_prompt = rmsnorm(xp, final_norm)
    y_sample = rmsnorm(xs, final_norm)
    return (y_prompt, y_sample, p_lru_conv, p_lru_h, p_ret, p_ckv, p_kpe, p_dk, p_dv,
            s_lru_conv, s_lru_h, s_ret, s_ckv, s_kpe, s_dk, s_dv)
```

```python
import functools
import math

import jax
import jax.numpy as jnp
from jax import lax
from jax.experimental import pallas as pl
from jax.experimental.pallas import tpu as pltpu

F32 = jnp.float32
BF16 = jnp.bfloat16

D_MODEL = 1024
PAGE = 128
GROUP_W = 256
LRU_W = 256
LRU_BLOCKS = 4
LRU_BW = 64
CONV_W = 4
LRU_C = 8.0
RET_HEADS = 4
RET_DV = 64
RET_DK = 32
MLA_HEADS = 4
MLA_V = 64
MLA_NOPE = 64
MLA_ROPE = 32
MLA_Q_RANK = 192
MLA_KV_RANK = 128
MLA_SCALE = (MLA_NOPE + MLA_ROPE) ** -0.5
DIFF_KV_HEADS = 2
DIFF_REP = 2
DIFF_V = 64
DIFF_D = 32
DIFF_SCALE = DIFF_D ** -0.5
MOE_GROUPS = 4
MOE_PER_GROUP = 4
MOE_EXPERTS = 16
MOE_FF = 256
ROPE_THETA = 10000.0
EPS = 1e-6

VMEM_LIMIT_BYTES = 56 * 1024 * 1024
LANES = 128

_IN_SIZES = (256, 256, 128, 128, 256, 256, 192, 128, 32, 256, 128, 128)
_IN_PAD = (256, 256, 128, 128, 256, 256, 256, 128, 128, 256, 128, 128)
_IN_WIDTH = sum(_IN_PAD)

RET_L = 256
LRU_TT = 256
FLASH_TQ = 512
FLASH_TK = 512
FLASH_CB = 256
BF16_ROWS = 16
LOG2E = math.log2(math.e)
DECODE_CHUNK = 2048


def _cparams(n_axes):
    return pltpu.CompilerParams(dimension_semantics=("arbitrary",) * n_axes,
                                vmem_limit_bytes=VMEM_LIMIT_BYTES)


def _dot(a, b):
    return jnp.dot(a, b, preferred_element_type=F32)


def _dot_nt(a, b):
    return lax.dot_general(a, b, (((1,), (1,)), ((), ())), preferred_element_type=F32)


def _split(a):
    hi = a.astype(BF16)
    return hi, (a - hi.astype(F32)).astype(BF16)


def _dot3(a, b_hi, b_lo):
    a_hi, a_lo = _split(a)
    return _dot(a_hi, b_hi) + (_dot(a_hi, b_lo) + _dot(a_lo, b_hi))


def _dot3v(a, b):
    return _dot3(a, *_split(b))


def _rms(x, g, width=None):
    w = x.shape[-1] if width is None else width
    ms = jnp.sum(x * x, axis=-1, keepdims=True) * (1.0 / w)
    return x * lax.rsqrt(ms + EPS) * g


def _in_proj_body(x_ref, n1_ref, win_ref, cos_ref, sin_ref, qn_ref, wuqn_ref, wuqr_ref, wukt_ref, kvn_ref,
                  lrux_ref, lrug_ref, rq_ref, rk_ref, rv_ref, rg_ref, qlat_ref, qpe_ref, ckv_ref, kpe_ref,
                  dq_ref, dk_ref, dv_ref):
    x = x_ref[...]
    h = _rms(x, n1_ref[...])
    z = _dot3(h, win_ref[0], win_ref[1])
    cos = cos_ref[...]
    sin = sin_ref[...]
    lane = lax.broadcasted_iota(jnp.int32, cos.shape, 1)
    first_half = (lane & 31) < 16

    def rope(v):
        swapped = jnp.where(first_half, pltpu.roll(v, LANES - 16, 1), pltpu.roll(v, 16, 1))
        return v * cos + swapped * sin

    lrux_ref[...] = z[:, 0:256]
    lrug_ref[...] = z[:, 256:512]
    rq_ref[...] = rope(z[:, 512:640])
    rk_ref[...] = rope(z[:, 640:768]) * (RET_DK ** -0.5)
    rv_ref[...] = z[:, 768:1024]
    rg_ref[...] = z[:, 1024:1280]
    cq = _rms(z[:, 1280:1536], qn_ref[...], MLA_Q_RANK)
    q_nope = _dot3(cq, wuqn_ref[0], wuqn_ref[1])
    qpe_ref[...] = rope(_dot3(cq, wuqr_ref[0], wuqr_ref[1]))
    for hh in range(MLA_HEADS):
        qlat_ref[:, hh * MLA_KV_RANK:(hh + 1) * MLA_KV_RANK] = _dot3(
            q_nope[:, hh * MLA_NOPE:(hh + 1) * MLA_NOPE], wukt_ref[0, hh], wukt_ref[1, hh])
    ckv_ref[...] = _rms(z[:, 1536:1664], kvn_ref[...])
    kpe_ref[...] = rope(z[:, 1664:1792])[:, 0:MLA_ROPE]
    dq_ref[...] = z[:, 1792:2048]
    dk_ref[...] = z[:, 2048:2176]
    dv_ref[...] = z[:, 2176:2304]


def _in_proj(x2d, n1, w_in_p, cos, sin, qn, wuqn, wuqr, wukt, kvn, tm):
    m = x2d.shape[0]
    tb = cos.shape[0] // tm
    row = lambda i: (i, 0)
    fix = lambda i: (0, 0)
    fix3 = lambda i: (0, 0, 0)
    out_w = (256, 256, 128, 128, 256, 256, 512, 128, 128, 32, 256, 128, 128)
    return pl.pallas_call(
        _in_proj_body,
        grid=(m // tm,),
        in_specs=[
            pl.BlockSpec((tm, D_MODEL), row),
            pl.BlockSpec((1, D_MODEL), fix),
            pl.BlockSpec((2, D_MODEL, _IN_WIDTH), fix3),
            pl.BlockSpec((tm, LANES), lambda i: (i % tb, 0)),
            pl.BlockSpec((tm, LANES), lambda i: (i % tb, 0)),
            pl.BlockSpec((1, 256), fix),
            pl.BlockSpec((2, 256, 256), fix3),
            pl.BlockSpec((2, 256, 128), fix3),
            pl.BlockSpec((2, MLA_HEADS, MLA_NOPE, MLA_KV_RANK), lambda i: (0, 0, 0, 0)),
            pl.BlockSpec((1, MLA_KV_RANK), fix),
        ],
        out_specs=[pl.BlockSpec((tm, w), row) for w in out_w],
        out_shape=[jax.ShapeDtypeStruct((m, w), F32) for w in out_w],
        compiler_params=_cparams(1),
        name="in_proj",
    )(x2d, n1, w_in_p, cos, sin, qn, wuqn, wuqr, wukt, kvn)


def _softplus(y):
    return jnp.maximum(y, 0.0) + jnp.log1p(jnp.exp(-jnp.abs(y)))


def _lru_gates(xc, wa_ref, ba_ref, wx_ref, bx_ref, lam_ref):
    r = jax.nn.sigmoid(_dot3(xc, wa_ref[0], wa_ref[1]) + ba_ref[...])
    i = jax.nn.sigmoid(_dot3(xc, wx_ref[0], wx_ref[1]) + bx_ref[...])
    log_a = -LRU_C * r * _softplus(-lam_ref[...])
    a = jnp.exp(log_a)
    th = jnp.tanh(log_a)
    b = jnp.sqrt(-2.0 * th / (1.0 - th)) * (i * xc)
    return a, b


def _lru_seq_body(ax_ref, ag_ref, buf0_ref, h0_ref, cw_ref, cb_ref, wa_ref, ba_ref, wx_ref, bx_ref, lam_ref,
                  y_ref, nbuf_ref, hl_ref, xbuf, hcar):
    t = pl.program_id(1)
    tt = ax_ref.shape[1]

    @pl.when(t == 0)
    def _():
        xbuf[0:8, :] = jnp.zeros((8, LRU_W), F32)
        xbuf[5:8, :] = buf0_ref[0]
        hcar[...] = h0_ref[0]

    xbuf[8:8 + tt, :] = ax_ref[0]
    cw = cw_ref[...]
    xc = cb_ref[...] + sum(xbuf[5 + k:5 + k + tt, :] * cw[k:k + 1, :] for k in range(CONV_W))
    a, b = _lru_gates(xc, wa_ref, ba_ref, wx_ref, bx_ref, lam_ref)
    rowi = lax.broadcasted_iota(jnp.int32, (tt, LRU_W), 0)
    d = 1
    while d < tt:
        keep = rowi >= d
        a_sh = jnp.where(keep, pltpu.roll(a, d, 0), 1.0)
        b_sh = jnp.where(keep, pltpu.roll(b, d, 0), 0.0)
        b = a * b_sh + b
        a = a * a_sh
        d *= 2
    hs = b + a * hcar[...]
    y_ref[0] = hs * jax.nn.gelu(ag_ref[0])
    hcar[...] = hs[tt - 1:tt, :]
    xbuf[0:8, :] = xbuf[tt:tt + 8, :]

    @pl.when(t == pl.num_programs(1) - 1)
    def _():
        nbuf_ref[0] = xbuf[5:8, :]
        hl_ref[0] = hs[tt - 1:tt, :]


def _lru_seq(ax, ag, buf0, h0, cw, cb, wa, ba, wx, bx, lam):
    b, s, _ = ax.shape
    tt = LRU_TT
    seq = lambda i, t: (i, t, 0)
    per_b = lambda i, t: (i, 0, 0)
    fix = lambda i, t: (0, 0)
    return pl.pallas_call(
        _lru_seq_body,
        grid=(b, s // tt),
        in_specs=[
            pl.BlockSpec((1, tt, LRU_W), seq), pl.BlockSpec((1, tt, LRU_W), seq),
            pl.BlockSpec((1, CONV_W - 1, LRU_W), per_b), pl.BlockSpec((1, 1, LRU_W), per_b),
            pl.BlockSpec((CONV_W, LRU_W), fix), pl.BlockSpec((1, LRU_W), fix),
            pl.BlockSpec((2, LRU_W, LRU_W), lambda i, t: (0, 0, 0)), pl.BlockSpec((1, LRU_W), fix),
            pl.BlockSpec((2, LRU_W, LRU_W), lambda i, t: (0, 0, 0)), pl.BlockSpec((1, LRU_W), fix),
            pl.BlockSpec((1, LRU_W), fix),
        ],
        out_specs=[pl.BlockSpec((1, tt, LRU_W), seq), pl.BlockSpec((1, CONV_W - 1, LRU_W), per_b),
                   pl.BlockSpec((1, 1, LRU_W), per_b)],
        out_shape=[jax.ShapeDtypeStruct((b, s, LRU_W), F32), jax.ShapeDtypeStruct((b, CONV_W - 1, LRU_W), F32),
                   jax.ShapeDtypeStruct((b, 1, LRU_W), F32)],
        scratch_shapes=[pltpu.VMEM((tt + 8, LRU_W), F32), pltpu.VMEM((1, LRU_W), F32)],
        compiler_params=_cparams(2),
        name="lru_seq",
    )(ax, ag, buf0, h0.reshape(b, 1, LRU_W), cw, cb, wa, ba, wx, bx, lam)


def _lru_step_body(ax_ref, ag_ref, b0_ref, b1_ref, b2_ref, h0_ref, cw_ref, cb_ref, wa_ref, ba_ref, wx_ref, bx_ref,
                   lam_ref, y_ref, hn_ref):
    cw = cw_ref[...]
    xc = (cb_ref[...] + b0_ref[...] * cw[0:1, :] + b1_ref[...] * cw[1:2, :] + b2_ref[...] * cw[2:3, :]
          + ax_ref[...] * cw[3:4, :])
    a, b = _lru_gates(xc, wa_ref, ba_ref, wx_ref, bx_ref, lam_ref)
    hs = b + a * h0_ref[...]
    y_ref[...] = hs * jax.nn.gelu(ag_ref[...])
    hn_ref[...] = hs


def _lru_step(ax, ag, buf, h0, cw, cb, wa, ba, wx, bx, lam):
    n = ax.shape[0]
    return pl.pallas_call(
        _lru_step_body,
        out_shape=[jax.ShapeDtypeStruct((n, LRU_W), F32), jax.ShapeDtypeStruct((n, LRU_W), F32)],
        compiler_params=pltpu.CompilerParams(vmem_limit_bytes=VMEM_LIMIT_BYTES),
        name="lru_step",
    )(ax, ag, buf[:, 0], buf[:, 1], buf[:, 2], h0, cw, cb, wa, ba, wx, bx, lam)


def _group_norm_gate(o, gn, g):
    mu = jnp.mean(o, axis=-1, keepdims=True)
    var = jnp.mean(jnp.square(o - mu), axis=-1, keepdims=True)
    return jax.nn.silu(g) * ((o - mu) * lax.rsqrt(var + EPS) * gn)


def _ret_seq_body(q_ref, kt_ref, v_ref, g_ref, s0_ref, dmat_ref, rdec_ref, cdec_ref, gl_ref, gn_ref,
                  y_ref, sn_ref, state):
    c = pl.program_id(1)

    @pl.when(c == 0)
    def _():
        state[...] = s0_ref[0]

    q = q_ref[0]
    kt = kt_ref[0]
    v = v_ref[0]
    g = g_ref[0]
    for hh in range(RET_HEADS):
        qh = q[:, hh * RET_DK:(hh + 1) * RET_DK]
        kth = kt[hh * RET_DK:(hh + 1) * RET_DK, :]
        vh = _split(v[:, hh * RET_DV:(hh + 1) * RET_DV])
        s_old = state[hh]
        s = _dot3v(qh, kth) * dmat_ref[hh]
        o = _dot3(s, *vh) + _dot3v(qh, s_old) * rdec_ref[hh]
        state[hh] = gl_ref[hh] * s_old + _dot3(kth * cdec_ref[hh], *vh)
        sl = slice(hh * RET_DV, (hh + 1) * RET_DV)
        y_ref[0, :, sl] = _group_norm_gate(o, gn_ref[:, sl], g[:, sl])

    @pl.when(c == pl.num_programs(1) - 1)
    def _():
        sn_ref[0] = state[...]


def _ret_tables(chunk):
    logg = jnp.log1p(-jnp.exp2(-5.0 - jnp.arange(RET_HEADS, dtype=F32)))
    idx = jnp.arange(chunk, dtype=F32)
    diff = idx[:, None] - idx[None, :]
    dmat = jnp.where(diff >= 0, jnp.exp(jnp.maximum(diff, 0.0)[None] * logg[:, None, None]), 0.0)
    rdec = jnp.exp((idx + 1.0)[None, :, None] * logg[:, None, None])
    cdec = jnp.exp((chunk - 1.0 - idx)[None, None, :] * logg[:, None, None])
    gl = jnp.broadcast_to(jnp.exp(chunk * logg)[:, None, None], (RET_HEADS, RET_DK, RET_DV))
    return dmat, rdec, cdec, gl, logg


def _ret_seq(q, kt, v, g, s0, gn):
    b, s, _ = q.shape
    ch = RET_L
    dmat, rdec, cdec, gl, _ = _ret_tables(ch)
    seq = lambda i, c: (i, c, 0)
    fix3 = lambda i, c: (0, 0, 0)
    return pl.pallas_call(
        _ret_seq_body,
        grid=(b, s // ch),
        in_specs=[
            pl.BlockSpec((1, ch, RET_HEADS * RET_DK), seq),
            pl.BlockSpec((1, RET_HEADS * RET_DK, ch), lambda i, c: (i, 0, c)),
            pl.BlockSpec((1, ch, GROUP_W), seq), pl.BlockSpec((1, ch, GROUP_W), seq),
            pl.BlockSpec((1, RET_HEADS, RET_DK, RET_DV), lambda i, c: (i, 0, 0, 0)),
            pl.BlockSpec((RET_HEADS, ch, ch), fix3), pl.BlockSpec((RET_HEADS, ch, 1), fix3),
            pl.BlockSpec((RET_HEADS, 1, ch), fix3), pl.BlockSpec((RET_HEADS, RET_DK, RET_DV), fix3),
            pl.BlockSpec((1, GROUP_W), lambda i, c: (0, 0)),
        ],
        out_specs=[pl.BlockSpec((1, ch, GROUP_W), seq),
                   pl.BlockSpec((1, RET_HEADS, RET_DK, RET_DV), lambda i, c: (i, 0, 0, 0))],
        out_shape=[jax.ShapeDtypeStruct((b, s, GROUP_W), F32),
                   jax.ShapeDtypeStruct((b, RET_HEADS, RET_DK, RET_DV), F32)],
        scratch_shapes=[pltpu.VMEM((RET_HEADS, RET_DK, RET_DV), F32)],
        compiler_params=_cparams(2),
        name="ret_seq",
    )(q, kt, v, g, s0, dmat, rdec, cdec, gl, gn)


def _ret_step_body(q_ref, k_ref, v_ref, g_ref, s0_ref, dec_ref, gn_ref, y_ref, sn_ref):
    q = q_ref[...]
    k = k_ref[...]
    v = v_ref[...]
    s0 = s0_ref[...]
    dec = dec_ref[...]
    qk = jnp.sum(q * k, axis=1, keepdims=True)
    o = qk * v + jnp.sum(q * s0, axis=1, keepdims=True) * dec
    sn_ref[...] = dec * s0 + k * v
    y_ref[...] = _group_norm_gate(o, gn_ref[...], g_ref[...])


def _ret_step(q, k, v, g, s0, gn):
    n = q.shape[0]
    x = n * RET_HEADS
    xb = 64
    _, _, _, _, logg = _ret_tables(1)
    dec = jnp.broadcast_to(jnp.exp(logg)[None, :, None, None], (xb // RET_HEADS, RET_HEADS, 1, RET_DV))
    dec = dec.reshape(xb, 1, RET_DV)
    gn3 = jnp.broadcast_to(gn.reshape(1, RET_HEADS, 1, RET_DV), (xb // RET_HEADS, RET_HEADS, 1, RET_DV))
    gn3 = gn3.reshape(xb, 1, RET_DV)
    blk = lambda i: (i, 0, 0)
    fix = lambda i: (0, 0, 0)
    y, sn = pl.pallas_call(
        _ret_step_body,
        grid=(x // xb,),
        in_specs=[pl.BlockSpec((xb, RET_DK, 1), blk), pl.BlockSpec((xb, RET_DK, 1), blk),
                  pl.BlockSpec((xb, 1, RET_DV), blk), pl.BlockSpec((xb, 1, RET_DV), blk),
                  pl.BlockSpec((xb, RET_DK, RET_DV), blk),
                  pl.BlockSpec((xb, 1, RET_DV), fix), pl.BlockSpec((xb, 1, RET_DV), fix)],
        out_specs=[pl.BlockSpec((xb, 1, RET_DV), blk), pl.BlockSpec((xb, RET_DK, RET_DV), blk)],
        out_shape=[jax.ShapeDtypeStruct((x, 1, RET_DV), F32), jax.ShapeDtypeStruct((x, RET_DK, RET_DV), F32)],
        compiler_params=_cparams(1),
        name="ret_step",
    )(q.reshape(x, RET_DK, 1), k.reshape(x, RET_DK, 1), v.reshape(x, 1, RET_DV), g.reshape(x, 1, RET_DV),
      s0.reshape(x, RET_DK, RET_DV), dec, gn3)
    return y.reshape(n, GROUP_W), sn.reshape(n, RET_HEADS, RET_DK, RET_DV)


def _diff_lambda(lq1_ref, lk1_ref, lq2_ref, lk2_ref, lam_init):
    return (jnp.exp(jnp.sum(lq1_ref[...] * lk1_ref[...], axis=-1, keepdims=True))
            - jnp.exp(jnp.sum(lq2_ref[...] * lk2_ref[...], axis=-1, keepdims=True)) + lam_init)


def _flash_body(*refs, tq, tk, e, diff, lam_init):
    n_in = 8 if diff else 3
    q_ref, k_ref, vt_ref = refs[:3]
    o_ref = refs[n_in]
    scr = refs[n_in + 1:]
    n_blk = len(scr) // 2
    m_scrs, acc_scrs = scr[:n_blk], scr[n_blk:]
    if diff:
        lq1_ref, lk1_ref, lq2_ref, lk2_ref, sub_ref = refs[3:8]
    qi = pl.program_id(2)
    cols = q_ref.shape[3]
    cb_w = cols // n_blk
    n_full = (qi * tq) // tk
    for m_scr, acc_scr in zip(m_scrs, acc_scrs):
        m_scr[...] = jnp.full(m_scr.shape, -jnp.inf, F32)
        acc_scr[...] = jnp.zeros(acc_scr.shape, F32)

    def step(j, masked):
        kj = k_ref[0, 0, j]
        vj = vt_ref[0, 0, j]
        sts = [_dot_nt(kj, q_ref[0, 0, 0, cb * cb_w:(cb + 1) * cb_w, :]) for cb in range(n_blk)]
        for cb, (st, m_scr, acc_scr) in enumerate(zip(sts, m_scrs, acc_scrs)):
            if masked:
                kpos = j * tk + lax.broadcasted_iota(jnp.int32, st.shape, 0)
                qpos = qi * tq + ((cb * cb_w + lax.broadcasted_iota(jnp.int32, st.shape, 1)) & (tq - 1))
                st = jnp.where(kpos <= qpos, st, -jnp.inf)
            m_old = m_scr[...]
            m_new = jnp.maximum(m_old, jnp.max(st, axis=0, keepdims=True))
            p = jnp.exp2(st - m_new).astype(BF16)
            acc_scr[...] = jnp.exp2(m_old - m_new) * acc_scr[...] + _dot(vj, p)
            m_scr[...] = m_new

    def full_step(j, carry):
        step(j, False)
        return carry

    lax.fori_loop(0, n_full, full_step, 0)
    step(n_full, True)
    acc = jnp.concatenate([a[...] for a in acc_scrs], axis=1) if n_blk > 1 else acc_scrs[0][...]
    o = acc[0:e, :] / acc[e:e + 1, :]
    if diff:
        lam = _diff_lambda(lq1_ref, lk1_ref, lq2_ref, lk2_ref, lam_init)
        half = cols // 2
        a = o[:, 0:half] - lam * o[:, half:cols]
        ms = jnp.mean(a * a, axis=0, keepdims=True)
        o_ref[0, 0, 0] = a * lax.rsqrt(ms + EPS) * sub_ref[...] * (1.0 - lam_init)
    else:
        o_ref[0, 0, 0] = o


def _flash(q, k, vt, extra=None, lam_init=0.0):
    b, hk, nq, cols, dk = q.shape
    nk, e = k.shape[2], vt.shape[3]
    ea = e + BF16_ROWS
    tail = jnp.zeros(vt.shape[:3] + (BF16_ROWS, vt.shape[4]), BF16).at[:, :, :, 0, :].set(1.0)
    vt = jnp.concatenate([vt, tail], axis=3)
    diff = extra is not None
    out_cols = cols // 2 if diff else cols
    in_specs = [
        pl.BlockSpec((1, 1, 1, cols, dk), lambda i, h, j: (i, h, j, 0, 0)),
        pl.BlockSpec((1, 1, nk, FLASH_TK, dk), lambda i, h, j: (i, h, 0, 0, 0)),
        pl.BlockSpec((1, 1, nk, ea, FLASH_TK), lambda i, h, j: (i, h, 0, 0, 0)),
    ]
    args = [q, k, vt]
    if diff:
        in_specs += [pl.BlockSpec((1, DIFF_D), lambda i, h, j: (0, 0))] * 4
        in_specs += [pl.BlockSpec((DIFF_V, 1), lambda i, h, j: (0, 0))]
        args += list(extra)
    return pl.pallas_call(
        functools.partial(_flash_body, tq=FLASH_TQ, tk=FLASH_TK, e=e, diff=diff, lam_init=lam_init),
        grid=(b, hk, nq),
        in_specs=in_specs,
        out_specs=pl.BlockSpec((1, 1, 1, e, out_cols), lambda i, h, j: (i, h, j, 0, 0)),
        out_shape=jax.ShapeDtypeStruct((b, hk, nq, e, out_cols), F32),
        scratch_shapes=([pltpu.VMEM((1, FLASH_CB), F32)] * (cols // FLASH_CB)
                        + [pltpu.VMEM((ea, FLASH_CB), F32)] * (cols // FLASH_CB)),
        compiler_params=_cparams(3),
        name="flash_diff" if diff else "flash_mla",
    )(*args)


def _page_copies(pt_ref, n, slot, srcs, bufs, sems, n_pages, pick):
    copies = []
    for j in range(n_pages):
        page = pt_ref[n, j]
        for src, buf, sem, row_major in zip(srcs, bufs, sems, (pick == "mla", False)):
            if row_major:
                dst = buf.at[slot, pl.ds(j * PAGE, PAGE), :]
            else:
                dst = buf.at[slot, :, pl.ds(j * PAGE, PAGE)]
            copies.append(pltpu.make_async_copy(src(page), dst, sem.at[slot]))
    return copies


def _mla_decode_body(pt_ref, ql_ref, qp_ref, cn_ref, kn_ref, ckv_hbm, kpet_hbm, o_ref, cbuf, kbuf, c16, sem_c, sem_k,
                     *, layer, n_pages):
    n = pl.program_id(0)
    slot = n % 2
    srcs = (lambda pg: ckv_hbm.at[layer, pg], lambda pg: kpet_hbm.at[layer, pg])

    def copies(nn, sl):
        return _page_copies(pt_ref, nn, sl, srcs, (cbuf, kbuf), (sem_c, sem_k), n_pages, "mla")

    @pl.when(n == 0)
    def _():
        for cp in copies(0, 0):
            cp.start()

    @pl.when(n + 1 < pl.num_programs(0))
    def _():
        for cp in copies(n + 1, 1 - slot):
            cp.start()

    for cp in copies(n, slot):
        cp.wait()

    ql = ql_ref[0]
    qp = qp_ref[0]
    cn = cn_ref[0]
    kn = kn_ref[0]
    qlb, qpb = ql.astype(BF16), qp.astype(BF16)
    s_own = (jnp.sum(ql * cn, axis=-1, keepdims=True) + jnp.sum(qp * kn, axis=-1, keepdims=True)) * MLA_SCALE
    n_chunks = (n_pages * PAGE) // DECODE_CHUNK
    scores = []
    for c in range(n_chunks):
        lo = c * DECODE_CHUNK
        c16[lo:lo + DECODE_CHUNK, :] = cbuf[slot, lo:lo + DECODE_CHUNK, :].astype(BF16)
        s = _dot_nt(qlb, c16[lo:lo + DECODE_CHUNK, :]) + _dot(qpb, kbuf[slot, :, lo:lo + DECODE_CHUNK].astype(BF16))
        scores.append(s * MLA_SCALE)
    m = s_own
    for s in scores:
        m = jnp.maximum(m, jnp.max(s, axis=-1, keepdims=True))
    p_own = jnp.exp(s_own - m)
    den = p_own
    acc = p_own * cn
    for c, s in enumerate(scores):
        lo = c * DECODE_CHUNK
        p = jnp.exp(s - m)
        den = den + jnp.sum(p, axis=-1, keepdims=True)
        acc = acc + _dot(p.astype(BF16), c16[lo:lo + DECODE_CHUNK, :])
    o_ref[0] = acc / den


def _mla_decode(page_table, q_lat, q_pe, ckv_new, kpe_new, cache_ckv, cache_kpet, layer):
    n, n_pages = page_table.shape
    keys = n_pages * PAGE
    blk = lambda i, pt: (i, 0, 0)
    grid_spec = pltpu.PrefetchScalarGridSpec(
        num_scalar_prefetch=1,
        grid=(n,),
        in_specs=[pl.BlockSpec((1, MLA_HEADS, MLA_KV_RANK), blk), pl.BlockSpec((1, MLA_HEADS, MLA_ROPE), blk),
                  pl.BlockSpec((1, 1, MLA_KV_RANK), blk), pl.BlockSpec((1, 1, MLA_ROPE), blk),
                  pl.BlockSpec(memory_space=pl.ANY), pl.BlockSpec(memory_space=pl.ANY)],
        out_specs=pl.BlockSpec((1, MLA_HEADS, MLA_KV_RANK), blk),
        scratch_shapes=[pltpu.VMEM((2, keys, MLA_KV_RANK), F32), pltpu.VMEM((2, MLA_ROPE, keys), F32),
                        pltpu.VMEM((keys, MLA_KV_RANK), BF16),
                        pltpu.SemaphoreType.DMA((2,)), pltpu.SemaphoreType.DMA((2,))],
    )
    return pl.pallas_call(
        functools.partial(_mla_decode_body, layer=layer, n_pages=n_pages),
        grid_spec=grid_spec,
        out_shape=jax.ShapeDtypeStruct((n, MLA_HEADS, MLA_KV_RANK), F32),
        compiler_params=_cparams(1),
        name="mla_decode",
    )(page_table, q_lat.reshape(n, MLA_HEADS, MLA_KV_RANK), q_pe.reshape(n, MLA_HEADS, MLA_ROPE),
      ckv_new.reshape(n, 1, MLA_KV_RANK), kpe_new.reshape(n, 1, MLA_ROPE), cache_ckv, cache_kpet)


def _diff_decode_body(pt_ref, q_ref, kn_ref, vn_ref, lq1_ref, lk1_ref, lq2_ref, lk2_ref, sub_ref, kt_hbm, vt_hbm,
                      o_ref, kbuf, vbuf, sem_k, sem_v, *, layer, n_pages, lam_init):
    n = pl.program_id(0)
    g = pl.program_id(1)
    step = n * DIFF_KV_HEADS + g
    slot = step % 2
    srcs_of = lambda gg: (lambda pg: kt_hbm.at[layer, pg, gg], lambda pg: vt_hbm.at[layer, pg, gg])

    def copies(nn, gg, sl):
        return _page_copies(pt_ref, nn, sl, srcs_of(gg), (kbuf, vbuf), (sem_k, sem_v), n_pages, "diff")

    @pl.when(step == 0)
    def _():
        for cp in copies(0, 0, 0):
            cp.start()

    @pl.when(step + 1 < pl.num_programs(0) * DIFF_KV_HEADS)
    def _():
        nxt = step + 1
        for cp in copies(nxt // DIFF_KV_HEADS, nxt % DIFF_KV_HEADS, 1 - slot):
            cp.start()

    for cp in copies(n, g, slot):
        cp.wait()

    q = q_ref[0, 0]
    kn = kn_ref[0, 0]
    vn = vn_ref[0, 0]
    qb = q.astype(BF16)
    s_own = jnp.sum(q * kn, axis=-1, keepdims=True) * DIFF_SCALE
    n_chunks = (n_pages * PAGE) // DECODE_CHUNK
    scores = []
    for c in range(n_chunks):
        lo = c * DECODE_CHUNK
        scores.append(_dot(qb, kbuf[slot, :, lo:lo + DECODE_CHUNK].astype(BF16)) * DIFF_SCALE)
    m = s_own
    for s in scores:
        m = jnp.maximum(m, jnp.max(s, axis=-1, keepdims=True))
    p_own = jnp.exp(s_own - m)
    den = p_own
    acc = p_own * vn
    for c, s in enumerate(scores):
        lo = c * DECODE_CHUNK
        p = jnp.exp(s - m)
        den = den + jnp.sum(p, axis=-1, keepdims=True)
        acc = acc + _dot_nt(p.astype(BF16), vbuf[slot, :, lo:lo + DECODE_CHUNK].astype(BF16))
    o = acc / den
    lam = _diff_lambda(lq1_ref, lk1_ref, lq2_ref, lk2_ref, lam_init)
    for r in range(DIFF_REP):
        a = o[2 * r:2 * r + 1, :] - lam * o[2 * r + 1:2 * r + 2, :]
        o_ref[0, 0, r:r + 1, :] = _rms(a, sub_ref[...]) * (1.0 - lam_init)


def _diff_decode(page_table, qbd, k_new, v_new, lam_refs, subln, cache_kt, cache_vt, layer, lam_init):
    n, n_pages = page_table.shape
    keys = n_pages * PAGE
    blk = lambda i, g, pt: (i, g, 0, 0)
    fix = lambda i, g, pt: (0, 0)
    grid_spec = pltpu.PrefetchScalarGridSpec(
        num_scalar_prefetch=1,
        grid=(n, DIFF_KV_HEADS),
        in_specs=[pl.BlockSpec((1, 1, 2 * DIFF_REP, 2 * DIFF_D), blk), pl.BlockSpec((1, 1, 1, 2 * DIFF_D), blk),
                  pl.BlockSpec((1, 1, 1, DIFF_V), blk)]
        + [pl.BlockSpec((1, DIFF_D), fix)] * 4 + [pl.BlockSpec((1, DIFF_V), fix)]
        + [pl.BlockSpec(memory_space=pl.ANY), pl.BlockSpec(memory_space=pl.ANY)],
        out_specs=pl.BlockSpec((1, 1, DIFF_REP, DIFF_V), blk),
        scratch_shapes=[pltpu.VMEM((2, 2 * DIFF_D, keys), F32), pltpu.VMEM((2, DIFF_V, keys), F32),
                        pltpu.SemaphoreType.DMA((2,)), pltpu.SemaphoreType.DMA((2,))],
    )
    return pl.pallas_call(
        functools.partial(_diff_decode_body, layer=layer, n_pages=n_pages, lam_init=lam_init),
        grid_spec=grid_spec,
        out_shape=jax.ShapeDtypeStruct((n, DIFF_KV_HEADS, DIFF_REP, DIFF_V), F32),
        compiler_params=_cparams(2),
        name="diff_decode",
    )(page_table, qbd, k_new, v_new, *lam_refs, subln, cache_kt, cache_vt)


def _post_body(x_ref, ya_ref, yb_ref, ol_ref, yd_ref, wuv_ref, wout_ref, n2_ref, wr_ref, br_ref,
               x1_ref, hm_ref, cmb_ref):
    ol = ol_ref[...]
    yc = jnp.concatenate(
        [_dot3(ol[:, hh * MLA_KV_RANK:(hh + 1) * MLA_KV_RANK], wuv_ref[0, hh], wuv_ref[1, hh])
         for hh in range(MLA_HEADS)], axis=1)
    y = None
    for gi, yg in enumerate((ya_ref[...], yb_ref[...], yc, yd_ref[...])):
        rows = slice(gi * GROUP_W, (gi + 1) * GROUP_W)
        part = _dot3(yg, wout_ref[0, rows, :], wout_ref[1, rows, :])
        y = part if y is None else y + part
    x1 = x_ref[...] + y
    x1_ref[...] = x1
    hm = _rms(x1, n2_ref[...])
    hm_ref[...] = hm
    logits = jnp.dot(hm, wr_ref[...], preferred_element_type=F32, precision=lax.Precision.HIGHEST) + br_ref[...]
    lane_i = lax.broadcasted_iota(jnp.int32, logits.shape, 1)
    lane = lane_i.astype(F32)
    far = float(LANES)
    gl = jnp.where((lane_i >= MOE_EXPERTS) & (lane_i < MOE_EXPERTS + MOE_GROUPS), logits, -jnp.inf)
    gmax = jnp.max(gl, axis=1, keepdims=True)
    gsel = jnp.min(jnp.where(gl == gmax, lane, far), axis=1, keepdims=True) - float(MOE_EXPERTS)
    gw = 1.0 / jnp.sum(jnp.exp(gl - gmax), axis=1, keepdims=True)
    lane_group = (lane_i >> 2).astype(F32)
    el = jnp.where((lane_i < MOE_EXPERTS) & (lane_group == gsel), logits, -jnp.inf)
    v1 = jnp.max(el, axis=1, keepdims=True)
    i1 = jnp.min(jnp.where(el == v1, lane, far), axis=1, keepdims=True)
    el2 = jnp.where(lane == i1, -jnp.inf, el)
    v2 = jnp.max(el2, axis=1, keepdims=True)
    i2 = jnp.min(jnp.where(el2 == v2, lane, far), axis=1, keepdims=True)
    e21 = jnp.exp(v2 - v1)
    w1 = 1.0 / (1.0 + e21)
    w2 = e21 / (1.0 + e21)
    cmb_ref[...] = jnp.where(lane == i1, w1 * gw, jnp.where(lane == i2, w2 * gw, 0.0))


def _post(x2d, ya, yb, ol, yd, wuv, wout, n2, wr, br, tm):
    m = x2d.shape[0]
    row = lambda i: (i, 0)
    fix = lambda i: (0, 0)
    return pl.pallas_call(
        _post_body,
        grid=(m // tm,),
        in_specs=[pl.BlockSpec((tm, D_MODEL), row), pl.BlockSpec((tm, GROUP_W), row), pl.BlockSpec((tm, GROUP_W), row),
                  pl.BlockSpec((tm, MLA_HEADS * MLA_KV_RANK), row), pl.BlockSpec((tm, GROUP_W), row),
                  pl.BlockSpec((2, MLA_HEADS, MLA_KV_RANK, MLA_V), lambda i: (0, 0, 0, 0)),
                  pl.BlockSpec((2, D_MODEL, D_MODEL), lambda i: (0, 0, 0)), pl.BlockSpec((1, D_MODEL), fix),
                  pl.BlockSpec((D_MODEL, LANES), fix), pl.BlockSpec((1, LANES), fix)],
        out_specs=[pl.BlockSpec((tm, D_MODEL), row), pl.BlockSpec((tm, D_MODEL), row), pl.BlockSpec((tm, LANES), row)],
        out_shape=[jax.ShapeDtypeStruct((m, D_MODEL), F32), jax.ShapeDtypeStruct((m, D_MODEL), F32),
                   jax.ShapeDtypeStruct((m, LANES), F32)],
        compiler_params=_cparams(1),
        name="post",
    )(x2d, ya, yb, ol, yd, wuv, wout, n2, wr, br)


def _moe_body(hm_ref, cmb_ref, x1_ref, wg_ref, wu_ref, wd_ref, fn_ref, o_ref, *, final, precise):
    e = pl.program_id(1)

    @pl.when(e == 0)
    def _():
        o_ref[...] = x1_ref[...]

    if precise:
        mm = lambda a, w_ref: _dot3(a, w_ref[0, 0], w_ref[1, 0])
    else:
        mm = lambda a, w_ref: _dot(a.astype(BF16), w_ref[0, 0])
    h = hm_ref[...]
    hg = mm(h, wg_ref)
    hu = mm(h, wu_ref)
    cmb = cmb_ref[...]
    lane = lax.broadcasted_iota(jnp.int32, cmb.shape, 1)
    c = jnp.sum(jnp.where(lane == e, cmb, 0.0), axis=1, keepdims=True)
    act = jax.nn.silu(hg) * hu * c
    o_ref[...] += mm(act, wd_ref)
    if final:
        @pl.when(e == pl.num_programs(1) - 1)
        def _():
            o_ref[...] = _rms(o_ref[...], fn_ref[...])


def _moe(hm, cmb, x1, wg, wu, wd, fn, tm, final, precise):
    m = hm.shape[0]
    halves = 2 if precise else 1
    row = lambda i, e: (i, 0)
    wsel = lambda i, e: (0, e, 0, 0)
    return pl.pallas_call(
        functools.partial(_moe_body, final=final, precise=precise),
        grid=(m // tm, MOE_EXPERTS),
        in_specs=[pl.BlockSpec((tm, D_MODEL), row), pl.BlockSpec((tm, LANES), row), pl.BlockSpec((tm, D_MODEL), row),
                  pl.BlockSpec((halves, 1, D_MODEL, MOE_FF), wsel), pl.BlockSpec((halves, 1, D_MODEL, MOE_FF), wsel),
                  pl.BlockSpec((halves, 1, MOE_FF, D_MODEL), wsel), pl.BlockSpec((1, D_MODEL), lambda i, e: (0, 0))],
        out_specs=pl.BlockSpec((tm, D_MODEL), row),
        out_shape=jax.ShapeDtypeStruct((m, D_MODEL), F32),
        compiler_params=_cparams(2),
        name="moe",
    )(hm, cmb, x1, wg, wu, wd, fn)


def _rope_tables(pos):
    half = MLA_ROPE // 2
    inv = ROPE_THETA ** (-jnp.arange(half, dtype=F32) / half)
    ang = pos.astype(F32)[:, None] * inv[None, :]
    c, s = jnp.cos(ang), jnp.sin(ang)
    return (jnp.tile(jnp.concatenate([c, c], axis=-1), (1, 4)),
            jnp.tile(jnp.concatenate([-s, s], axis=-1), (1, 4)))


def _pad_cols(w, sizes, pads):
    pieces, off = [], 0
    for sz, pd in zip(sizes, pads):
        piece = w[:, off:off + sz]
        if pd > sz:
            piece = jnp.pad(piece, ((0, 0), (0, pd - sz)))
        pieces.append(piece)
        off += sz
    return jnp.concatenate(pieces, axis=1)


def _block_diag(w):
    nb, bw, _ = w.shape
    out = jnp.zeros((nb * bw, nb * bw), w.dtype)
    for i in range(nb):
        out = out.at[i * bw:(i + 1) * bw, i * bw:(i + 1) * bw].set(w[i])
    return out


def _hilo(w):
    hi = w.astype(BF16)
    return jnp.stack([hi, (w - hi.astype(F32)).astype(BF16)])


def _layer_weights(l, P):
    w = {}
    w["n1"] = P["norm1"][l][None, :]
    w["w_in"] = _hilo(_pad_cols(P["w_in"][l], _IN_SIZES, _IN_PAD))
    w["cw"] = P["conv_w"][l]
    w["cb"] = P["conv_b"][l][None, :]
    w["wa"] = _hilo(_block_diag(P["lru_wa"][l]))
    w["ba"] = P["lru_ba"][l][None, :]
    w["wx"] = _hilo(_block_diag(P["lru_wx"][l]))
    w["bx"] = P["lru_bx"][l][None, :]
    w["lam"] = P["lru_lambda"][l][None, :]
    w["gn"] = P["ret_gn"][l][None, :]
    w["qn"] = jnp.pad(P["mla_q_norm"][l], (0, 256 - MLA_Q_RANK))[None, :]
    wuq = P["mla_w_uq"][l].reshape(MLA_Q_RANK, MLA_HEADS, MLA_NOPE + MLA_ROPE)
    pad_rows = ((0, 256 - MLA_Q_RANK), (0, 0))
    w["wuqn"] = _hilo(jnp.pad(wuq[:, :, :MLA_NOPE].reshape(MLA_Q_RANK, MLA_HEADS * MLA_NOPE), pad_rows))
    w["wuqr"] = _hilo(jnp.pad(wuq[:, :, MLA_NOPE:].reshape(MLA_Q_RANK, MLA_HEADS * MLA_ROPE), pad_rows))
    w["wukt"] = _hilo(jnp.transpose(P["mla_w_uk"][l], (1, 2, 0)))
    w["kvn"] = P["mla_kv_norm"][l][None, :]
    w["wuv"] = _hilo(jnp.transpose(P["mla_w_uv"][l], (1, 0, 2)))
    w["lam_refs"] = tuple(P[k][l][None, :] for k in ("diff_lq1", "diff_lk1", "diff_lq2", "diff_lk2"))
    w["subln"] = P["diff_subln"][l]
    w["w_out"] = _hilo(P["w_out"][l])
    w["n2"] = P["norm2"][l][None, :]
    wr = jnp.concatenate([P["moe_w_re"][l], P["moe_w_rg"][l]], axis=1)
    w["wr"] = jnp.pad(wr, ((0, 0), (0, LANES - wr.shape[1])))
    br = jnp.concatenate([P["moe_b_re"][l], P["moe_b_rg"][l]])
    w["br"] = jnp.pad(br, (0, LANES - br.shape[0]))[None, :]
    w["wg"] = _hilo(P["moe_w_gate"][l])
    w["wu"] = _hilo(P["moe_w_up"][l])
    w["wd"] = _hilo(P["moe_w_down"][l])
    return w


def _prompt_mixers(l, w, pieces, b, s, lam_init):
    (lrux, lrug, rq, rk, rv, rg, qlat, qpe, ckv, kpe, dq, dk, dv) = pieces
    tq, tk = FLASH_TQ, FLASH_TK
    nq, nk = s // tq, s // tk
    r3 = lambda a: a.reshape(b, s, a.shape[-1])
    zeros = lambda *sh: jnp.zeros(sh, F32)
    ya, nbuf, hlast = _lru_seq(r3(lrux), r3(lrug), zeros(b, CONV_W - 1, LRU_W), zeros(b, LRU_W), w["cw"], w["cb"],
                               w["wa"], w["ba"], w["wx"], w["bx"], w["lam"])
    yb, s_new = _ret_seq(r3(rq), jnp.swapaxes(r3(rk), 1, 2), r3(rv), r3(rg),
                         zeros(b, RET_HEADS, RET_DK, RET_DV), w["gn"])
    dk_mla = 2 * LANES
    zpad = lambda a: jnp.pad(a, [(0, 0)] * (a.ndim - 1) + [(0, dk_mla - MLA_KV_RANK - MLA_ROPE)])
    qcat = jnp.concatenate([qlat.reshape(b, s, MLA_HEADS, MLA_KV_RANK), qpe.reshape(b, s, MLA_HEADS, MLA_ROPE)], -1)
    qcat = zpad((qcat * (MLA_SCALE * LOG2E)).astype(BF16)).reshape(b, nq, tq, MLA_HEADS, dk_mla)
    qcat = jnp.transpose(qcat, (0, 1, 3, 2, 4)).reshape(b, 1, nq, MLA_HEADS * tq, dk_mla)
    kcat = zpad(jnp.concatenate([r3(ckv), r3(kpe)], -1).astype(BF16)).reshape(b, 1, nk, tk, dk_mla)
    vt = jnp.transpose(r3(ckv).astype(BF16).reshape(b, nk, tk, MLA_KV_RANK), (0, 1, 3, 2))[:, None]
    ot = _flash(qcat, kcat, vt)
    ol = jnp.transpose(ot.reshape(b, nq, MLA_KV_RANK, MLA_HEADS, tq), (0, 1, 4, 3, 2))
    ol = ol.reshape(b * s, MLA_HEADS * MLA_KV_RANK)
    q6 = (dq * (DIFF_SCALE * LOG2E)).reshape(b, s, DIFF_KV_HEADS, DIFF_REP, 2, DIFF_D)
    zq = jnp.zeros_like(q6[..., 0, :])
    qm = jnp.stack([jnp.concatenate([q6[..., 0, :], zq], -1), jnp.concatenate([zq, q6[..., 1, :]], -1)], axis=2)
    qm = qm.astype(BF16).reshape(b, nq, tq, 2, DIFF_KV_HEADS, DIFF_REP, 2 * DIFF_D)
    qm = jnp.transpose(qm, (0, 4, 1, 3, 5, 2, 6)).reshape(b, DIFF_KV_HEADS, nq, 2 * DIFF_REP * tq, 2 * DIFF_D)
    kd = jnp.transpose(dk.astype(BF16).reshape(b, nk, tk, DIFF_KV_HEADS, 2 * DIFF_D), (0, 3, 1, 2, 4))
    lane_pad = [(0, 0)] * 4 + [(0, LANES - 2 * DIFF_D)]
    qm, kd = jnp.pad(qm, lane_pad), jnp.pad(kd, lane_pad)
    vd = jnp.transpose(dv.astype(BF16).reshape(b, nk, tk, DIFF_KV_HEADS, DIFF_V), (0, 3, 1, 4, 2))
    yt = _flash(qm, kd, vd, extra=w["lam_refs"] + (w["subln"][:, None],), lam_init=lam_init)
    yd = jnp.transpose(yt.reshape(b, DIFF_KV_HEADS, nq, DIFF_V, DIFF_REP, tq), (0, 2, 5, 1, 4, 3))
    yd = yd.reshape(b * s, GROUP_W)
    new = (nbuf, hlast.reshape(b, LRU_W), s_new, r3(ckv), r3(kpe),
           dk.reshape(b, s, DIFF_KV_HEADS, 2 * DIFF_D), dv.reshape(b, s, DIFF_KV_HEADS, DIFF_V))
    return ya.reshape(b * s, LRU_W), yb.reshape(b * s, GROUP_W), ol, yd, new


def _sample_mixers(l, w, pieces, n, lru_buf, lru_h, ret_s, caches, page_table, lam_init):
    (lrux, lrug, rq, rk, rv, rg, qlat, qpe, ckv, kpe, dq, dk, dv) = pieces
    cache_ckv, cache_kpet, cache_kt, cache_vt = caches
    ya, h_new = _lru_step(lrux, lrug, lru_buf, lru_h, w["cw"], w["cb"], w["wa"], w["ba"], w["wx"], w["bx"], w["lam"])
    nbuf = jnp.concatenate([lru_buf[:, 1:], lrux[:, None, :]], axis=1)
    yb, s_new = _ret_step(rq, rk, rv, rg, ret_s, w["gn"])
    ol = _mla_decode(page_table, qlat, qpe, ckv, kpe, cache_ckv, cache_kpet, l).reshape(n, MLA_HEADS * MLA_KV_RANK)
    q5 = dq.reshape(n, DIFF_KV_HEADS, DIFF_REP, 2, DIFF_D)
    zq = jnp.zeros_like(q5[..., 0, :])
    qbd = jnp.stack([jnp.concatenate([q5[..., 0, :], zq], -1), jnp.concatenate([zq, q5[..., 1, :]], -1)], axis=3)
    qbd = qbd.reshape(n, DIFF_KV_HEADS, 2 * DIFF_REP, 2 * DIFF_D)
    yd = _diff_decode(page_table, qbd, dk.reshape(n, DIFF_KV_HEADS, 1, 2 * DIFF_D),
                      dv.reshape(n, DIFF_KV_HEADS, 1, DIFF_V), w["lam_refs"], w["subln"][None, :],
                      cache_kt, cache_vt, l, lam_init).reshape(n, GROUP_W)
    new = (nbuf, h_new, s_new, ckv.reshape(n, 1, MLA_KV_RANK), kpe.reshape(n, 1, MLA_ROPE),
           dk.reshape(n, 1, DIFF_KV_HEADS, 2 * DIFF_D), dv.reshape(n, 1, DIFF_KV_HEADS, DIFF_V))
    return ya, yb, ol, yd, new


def kernel(x_prompt, x_sample, state_lru_conv, state_lru_h, state_ret, cache_mla_ckv, cache_mla_kpe, cache_diff_k, cache_diff_v, page_table, norm1, w_in, conv_w, conv_b, lru_wa, lru_ba, lru_wx, lru_bx, lru_lambda, ret_gn, mla_q_norm, mla_w_uq, mla_kv_norm, mla_w_uk, mla_w_uv, diff_lq1, diff_lk1, diff_lq2, diff_lk2, diff_subln, w_out, norm2, moe_w_rg, moe_b_rg, moe_w_re, moe_b_re, moe_w_gate, moe_w_up, moe_w_down, final_norm):
    P = dict(norm1=norm1, w_in=w_in, conv_w=conv_w, conv_b=conv_b, lru_wa=lru_wa, lru_ba=lru_ba,
             lru_wx=lru_wx, lru_bx=lru_bx, lru_lambda=lru_lambda, ret_gn=ret_gn, mla_q_norm=mla_q_norm,
             mla_w_uq=mla_w_uq, mla_kv_norm=mla_kv_norm, mla_w_uk=mla_w_uk, mla_w_uv=mla_w_uv,
             diff_lq1=diff_lq1, diff_lk1=diff_lk1, diff_lq2=diff_lq2, diff_lk2=diff_lk2,
             diff_subln=diff_subln, w_out=w_out, norm2=norm2, moe_w_rg=moe_w_rg, moe_b_rg=moe_b_rg,
             moe_w_re=moe_w_re, moe_b_re=moe_b_re, moe_w_gate=moe_w_gate, moe_w_up=moe_w_up,
             moe_w_down=moe_w_down)
    depth = norm1.shape[0]
    b, s, _ = x_prompt.shape
    n = x_sample.shape[0]
    past_len = page_table.shape[1] * PAGE
    cos_p, sin_p = _rope_tables(jnp.arange(s, dtype=jnp.int32))
    cos_s, sin_s = _rope_tables(jnp.full((1,), past_len, jnp.int32))
    cos_s, sin_s = jnp.broadcast_to(cos_s, (n, LANES)), jnp.broadcast_to(sin_s, (n, LANES))
    caches = (cache_mla_ckv, jnp.swapaxes(cache_mla_kpe, 2, 3),
              jnp.transpose(cache_diff_k, (0, 1, 3, 4, 2)), jnp.transpose(cache_diff_v, (0, 1, 3, 4, 2)))
    fn = final_norm[None, :]
    tm_p = 512
    tm_in = 256
    xp = x_prompt.reshape(b * s, D_MODEL)
    xs = x_sample.reshape(n, D_MODEL)
    p_new, s_new = [], []
    for l in range(depth):
        w = _layer_weights(l, P)
        lam_init = 0.8 - 0.6 * math.exp(-0.3 * l)
        final = l == depth - 1
        proj = lambda x2d, cos, sin, tm: _in_proj(x2d, w["n1"], w["w_in"], cos, sin, w["qn"], w["wuqn"], w["wuqr"],
                                                  w["wukt"], w["kvn"], tm)
        ya, yb, ol, yd, st_p = _prompt_mixers(l, w, proj(xp, cos_p, sin_p, tm_in), b, s, lam_init)
        x1, hm, cmb = _post(xp, ya, yb, ol, yd, w["wuv"], w["w_out"], w["n2"], w["wr"], w["br"], tm_p)
        xp = _moe(hm, cmb, x1, w["wg"], w["wu"], w["wd"], fn, 1024, final, False)
        ya, yb, ol, yd, st_s = _sample_mixers(l, w, proj(xs, cos_s, sin_s, n), n, state_lru_conv[l], state_lru_h[l],
                                              state_ret[l], caches, page_table, lam_init)
        x1, hm, cmb = _post(xs, ya, yb, ol, yd, w["wuv"], w["w_out"], w["n2"], w["wr"], w["br"], n)
        xs = _moe(hm, cmb, x1, w["wg"], w["wu"], w["wd"], fn, n, final, True)
        p_new.append(st_p)
        s_new.append(st_s)
    outs_p = [jnp.stack([st[i] for st in p_new]) for i in range(7)]
    outs_s = [jnp.stack([st[i] for st in s_new]) for i in range(7)]
    return (xp.reshape(b, s, D_MODEL), xs.reshape(n, 1, D_MODEL), *outs_p, *outs_s)
```

```python
import functools
import math

import jax
import jax.numpy as jnp
from jax import lax
from jax.experimental import pallas as pl
from jax.experimental.pallas import tpu as pltpu

F32 = jnp.float32
BF16 = jnp.bfloat16

D_MODEL = 1024
PAGE = 128
GROUP_W = 256
LRU_W = 256
LRU_BLOCKS = 4
LRU_BW = 64
CONV_W = 4
LRU_C = 8.0
RET_HEADS = 4
RET_DV = 64
RET_DK = 32
MLA_HEADS = 4
MLA_V = 64
MLA_NOPE = 64
MLA_ROPE = 32
MLA_Q_RANK = 192
MLA_KV_RANK = 128
MLA_SCALE = (MLA_NOPE + MLA_ROPE) ** -0.5
DIFF_KV_HEADS = 2
DIFF_REP = 2
DIFF_V = 64
DIFF_D = 32
DIFF_SCALE = DIFF_D ** -0.5
MOE_GROUPS = 4
MOE_PER_GROUP = 4
MOE_EXPERTS = 16
MOE_FF = 256
ROPE_THETA = 10000.0
EPS = 1e-6

VMEM_LIMIT_BYTES = 56 * 1024 * 1024
LANES = 128

_IN_SIZES = (256, 256, 128, 128, 256, 256, 192, 128, 32, 256, 128, 128)
_IN_PAD = (256, 256, 128, 128, 256, 256, 256, 128, 128, 256, 128, 128)
_IN_WIDTH = sum(_IN_PAD)

RET_L = 256
LRU_TT = 256
FLASH_TQ = 512
FLASH_TK = 512
FLASH_CB = 256
BF16_ROWS = 16
LOG2E = math.log2(math.e)
DECODE_CHUNK = 2048


def _cparams(n_axes):
    return pltpu.CompilerParams(dimension_semantics=("arbitrary",) * n_axes,
                                vmem_limit_bytes=VMEM_LIMIT_BYTES)


def _dot(a, b):
    return jnp.dot(a, b, preferred_element_type=F32)


def _dot_nt(a, b):
    return lax.dot_general(a, b, (((1,), (1,)), ((), ())), preferred_element_type=F32)


def _split(a):
    hi = a.astype(BF16)
    return hi, (a - hi.astype(F32)).astype(BF16)


def _dot3(a, b_hi, b_lo):
    a_hi, a_lo = _split(a)
    return _dot(a_hi, b_hi) + (_dot(a_hi, b_lo) + _dot(a_lo, b_hi))


def _dot3v(a, b):
    return _dot3(a, *_split(b))


def _rms(x, g, width=None):
    w = x.shape[-1] if width is None else width
    ms = jnp.sum(x * x, axis=-1, keepdims=True) * (1.0 / w)
    return x * lax.rsqrt(ms + EPS) * g


def _ones_then_zero_rows(width):
    row = lax.broadcasted_iota(jnp.int32, (BF16_ROWS, width), 0)
    return jnp.where(row == 0, 1.0, 0.0).astype(BF16)


def _in_proj_body(x_ref, n1_ref, win_ref, cos_ref, sin_ref, qn_ref, wuqn_ref, wuqr_ref, wukt_ref, kvn_ref,
                  *out_refs, prompt):
    if prompt:
        (lrux_ref, lrug_ref, rq_ref, rkt_ref, rv_ref, rg_ref, ckv_ref, kpe_ref, dk_ref, dv_ref,
         qm_ref, km_ref, vtm_ref, qd_ref, kd_ref, vtd_ref) = out_refs
    else:
        (lrux_ref, lrug_ref, rq_ref, rk_ref, rv_ref, rg_ref, qlat_ref, qpe_ref, ckv_ref, kpe_ref,
         dq_ref, dk_ref, dv_ref) = out_refs
    x = x_ref[...]
    h = _rms(x, n1_ref[...])
    z = _dot3(h, win_ref[0], win_ref[1])
    cos = cos_ref[...]
    sin = sin_ref[...]
    lane = lax.broadcasted_iota(jnp.int32, cos.shape, 1)
    first_half = (lane & 31) < 16

    def rope(v):
        swapped = jnp.where(first_half, pltpu.roll(v, LANES - 16, 1), pltpu.roll(v, 16, 1))
        return v * cos + swapped * sin

    tm = x.shape[0]
    lrux_ref[...] = z[:, 0:256]
    lrug_ref[...] = z[:, 256:512]
    rq_ref[...] = rope(z[:, 512:640])
    rk = rope(z[:, 640:768]) * (RET_DK ** -0.5)
    rv_ref[...] = z[:, 768:1024]
    rg_ref[...] = z[:, 1024:1280]
    cq = _rms(z[:, 1280:1536], qn_ref[...], MLA_Q_RANK)
    q_nope = _dot3(cq, wuqn_ref[0], wuqn_ref[1])
    qpe = rope(_dot3(cq, wuqr_ref[0], wuqr_ref[1]))
    qlat = [_dot3(q_nope[:, hh * MLA_NOPE:(hh + 1) * MLA_NOPE], wukt_ref[0, hh], wukt_ref[1, hh])
            for hh in range(MLA_HEADS)]
    ckv = _rms(z[:, 1536:1664], kvn_ref[...])
    kpe_full = rope(z[:, 1664:1792])
    dq, dk, dv = z[:, 1792:2048], z[:, 2048:2176], z[:, 2176:2304]
    ckv_ref[...] = ckv
    kpe_ref[...] = kpe_full[:, 0:MLA_ROPE]
    dk_ref[...] = dk
    dv_ref[...] = dv
    if not prompt:
        rk_ref[...] = rk
        qpe_ref[...] = qpe
        for hh in range(MLA_HEADS):
            qlat_ref[:, hh * MLA_KV_RANK:(hh + 1) * MLA_KV_RANK] = qlat[hh]
        dq_ref[...] = dq
        return
    rkt_ref[0] = rk.T
    zeros = lambda w: jnp.zeros((tm, w), F32)
    c_mla = MLA_SCALE * LOG2E
    for hh in range(MLA_HEADS):
        pe = qpe[:, hh * MLA_ROPE:(hh + 1) * MLA_ROPE] * c_mla
        qm_ref[0, hh, :, 0:LANES] = (qlat[hh] * c_mla).astype(BF16)
        qm_ref[0, hh, :, LANES:2 * LANES] = jnp.concatenate([pe, zeros(LANES - MLA_ROPE)], axis=1).astype(BF16)
    km_ref[:, 0:LANES] = ckv.astype(BF16)
    km_ref[:, LANES:2 * LANES] = kpe_full.astype(BF16)
    vtm_ref[0, 0:MLA_KV_RANK, :] = ckv.T.astype(BF16)
    vtm_ref[0, MLA_KV_RANK:MLA_KV_RANK + BF16_ROWS, :] = _ones_then_zero_rows(tm)
    c_diff = DIFF_SCALE * LOG2E
    dvt = dv.T
    for g in range(DIFF_KV_HEADS):
        for r in range(DIFF_REP):
            for mm in range(2):
                lo = ((g * DIFF_REP + r) * 2 + mm) * DIFF_D
                parts = [zeros(mm * DIFF_D)] if mm else []
                parts += [dq[:, lo:lo + DIFF_D] * c_diff, zeros(LANES - (mm + 1) * DIFF_D)]
                qd_ref[0, g, 0, mm * DIFF_REP + r] = jnp.concatenate(parts, axis=1).astype(BF16)
        kd_ref[0, g, 0] = jnp.concatenate([dk[:, g * 2 * DIFF_D:(g + 1) * 2 * DIFF_D], zeros(LANES - 2 * DIFF_D)],
                                          axis=1).astype(BF16)
        vtd_ref[0, g, 0, 0:DIFF_V, :] = dvt[g * DIFF_V:(g + 1) * DIFF_V, :].astype(BF16)
        vtd_ref[0, g, 0, DIFF_V:DIFF_V + BF16_ROWS, :] = _ones_then_zero_rows(tm)


def _in_proj(x2d, n1, w_in_p, cos, sin, qn, wuqn, wuqr, wukt, kvn, tm, seq=None):
    m = x2d.shape[0]
    tb = cos.shape[0] // tm
    row = lambda i: (i, 0)
    fix = lambda i: (0, 0)
    fix3 = lambda i: (0, 0, 0)
    prompt = seq is not None
    if prompt:
        nb, ts = m // seq, seq // tm
        hq, hk = FLASH_TQ // tm, FLASH_TK // tm
        nq, nk = seq // FLASH_TQ, seq // FLASH_TK
        ea_m, ea_d = MLA_KV_RANK + BF16_ROWS, DIFF_V + BF16_ROWS
        f32_w = (256, 256, 128, None, 256, 256, 128, 32, 128, 128)
        out_specs = [pl.BlockSpec((tm, w), row) if w else
                     pl.BlockSpec((1, RET_HEADS * RET_DK, tm), lambda i: (i // ts, 0, i % ts)) for w in f32_w]
        out_shape = [jax.ShapeDtypeStruct((m, w), F32) if w else
                     jax.ShapeDtypeStruct((nb, RET_HEADS * RET_DK, seq), F32) for w in f32_w]
        out_specs += [
            pl.BlockSpec((1, MLA_HEADS, tm, 2 * LANES), lambda i: (i // hq, 0, i % hq, 0)),
            pl.BlockSpec((tm, 2 * LANES), row),
            pl.BlockSpec((1, ea_m, tm), lambda i: (i // hk, 0, i % hk)),
            pl.BlockSpec((1, DIFF_KV_HEADS, 1, 2 * DIFF_REP, tm, LANES),
                         lambda i: (i // ts, 0, (i % ts) // hq, 0, i % hq, 0)),
            pl.BlockSpec((1, DIFF_KV_HEADS, 1, tm, LANES), lambda i: (i // ts, 0, (i % ts) // hk, i % hk, 0)),
            pl.BlockSpec((1, DIFF_KV_HEADS, 1, ea_d, tm), lambda i: (i // ts, 0, (i % ts) // hk, 0, i % hk)),
        ]
        out_shape += [
            jax.ShapeDtypeStruct((nb * nq, MLA_HEADS, FLASH_TQ, 2 * LANES), BF16),
            jax.ShapeDtypeStruct((m, 2 * LANES), BF16),
            jax.ShapeDtypeStruct((nb * nk, ea_m, FLASH_TK), BF16),
            jax.ShapeDtypeStruct((nb, DIFF_KV_HEADS, nq, 2 * DIFF_REP, FLASH_TQ, LANES), BF16),
            jax.ShapeDtypeStruct((nb, DIFF_KV_HEADS, nk, FLASH_TK, LANES), BF16),
            jax.ShapeDtypeStruct((nb, DIFF_KV_HEADS, nk, ea_d, FLASH_TK), BF16),
        ]
    else:
        out_w = (256, 256, 128, 128, 256, 256, 512, 128, 128, 32, 256, 128, 128)
        out_specs = [pl.BlockSpec((tm, w), row) for w in out_w]
        out_shape = [jax.ShapeDtypeStruct((m, w), F32) for w in out_w]
    return pl.pallas_call(
        functools.partial(_in_proj_body, prompt=prompt),
        grid=(m // tm,),
        in_specs=[
            pl.BlockSpec((tm, D_MODEL), row),
            pl.BlockSpec((1, D_MODEL), fix),
            pl.BlockSpec((2, D_MODEL, _IN_WIDTH), fix3),
            pl.BlockSpec((tm, LANES), lambda i: (i % tb, 0)),
            pl.BlockSpec((tm, LANES), lambda i: (i % tb, 0)),
            pl.BlockSpec((1, 256), fix),
            pl.BlockSpec((2, 256, 256), fix3),
            pl.BlockSpec((2, 256, 128), fix3),
            pl.BlockSpec((2, MLA_HEADS, MLA_NOPE, MLA_KV_RANK), lambda i: (0, 0, 0, 0)),
            pl.BlockSpec((1, MLA_KV_RANK), fix),
        ],
        out_specs=out_specs,
        out_shape=out_shape,
        compiler_params=_cparams(1),
        name="in_proj",
    )(x2d, n1, w_in_p, cos, sin, qn, wuqn, wuqr, wukt, kvn)


def _softplus(y):
    return jnp.maximum(y, 0.0) + jnp.log1p(jnp.exp(-jnp.abs(y)))


def _lru_gates(xc, wa_ref, ba_ref, wx_ref, bx_ref, lam_ref):
    r = jax.nn.sigmoid(_dot3(xc, wa_ref[0], wa_ref[1]) + ba_ref[...])
    i = jax.nn.sigmoid(_dot3(xc, wx_ref[0], wx_ref[1]) + bx_ref[...])
    log_a = -LRU_C * r * _softplus(-lam_ref[...])
    a = jnp.exp(log_a)
    th = jnp.tanh(log_a)
    b = jnp.sqrt(-2.0 * th / (1.0 - th)) * (i * xc)
    return a, b


def _lru_seq_body(ax_ref, ag_ref, buf0_ref, h0_ref, cw_ref, cb_ref, wa_ref, ba_ref, wx_ref, bx_ref, lam_ref,
                  y_ref, nbuf_ref, hl_ref, xbuf, hcar):
    t = pl.program_id(1)
    tt = ax_ref.shape[1]

    @pl.when(t == 0)
    def _():
        xbuf[0:8, :] = jnp.zeros((8, LRU_W), F32)
        xbuf[5:8, :] = buf0_ref[0]
        hcar[...] = h0_ref[0]

    xbuf[8:8 + tt, :] = ax_ref[0]
    cw = cw_ref[...]
    xc = cb_ref[...] + sum(xbuf[5 + k:5 + k + tt, :] * cw[k:k + 1, :] for k in range(CONV_W))
    a, b = _lru_gates(xc, wa_ref, ba_ref, wx_ref, bx_ref, lam_ref)
    rowi = lax.broadcasted_iota(jnp.int32, (tt, LRU_W), 0)
    d = 1
    while d < tt:
        keep = rowi >= d
        a_sh = jnp.where(keep, pltpu.roll(a, d, 0), 1.0)
        b_sh = jnp.where(keep, pltpu.roll(b, d, 0), 0.0)
        b = a * b_sh + b
        a = a * a_sh
        d *= 2
    hs = b + a * hcar[...]
    y_ref[0] = hs * jax.nn.gelu(ag_ref[0])
    hcar[...] = hs[tt - 1:tt, :]
    xbuf[0:8, :] = xbuf[tt:tt + 8, :]

    @pl.when(t == pl.num_programs(1) - 1)
    def _():
        nbuf_ref[0] = xbuf[5:8, :]
        hl_ref[0] = hs[tt - 1:tt, :]


def _lru_seq(ax, ag, buf0, h0, cw, cb, wa, ba, wx, bx, lam):
    b, s, _ = ax.shape
    tt = LRU_TT
    seq = lambda i, t: (i, t, 0)
    per_b = lambda i, t: (i, 0, 0)
    fix = lambda i, t: (0, 0)
    return pl.pallas_call(
        _lru_seq_body,
        grid=(b, s // tt),
        in_specs=[
            pl.BlockSpec((1, tt, LRU_W), seq), pl.BlockSpec((1, tt, LRU_W), seq),
            pl.BlockSpec((1, CONV_W - 1, LRU_W), per_b), pl.BlockSpec((1, 1, LRU_W), per_b),
            pl.BlockSpec((CONV_W, LRU_W), fix), pl.BlockSpec((1, LRU_W), fix),
            pl.BlockSpec((2, LRU_W, LRU_W), lambda i, t: (0, 0, 0)), pl.BlockSpec((1, LRU_W), fix),
            pl.BlockSpec((2, LRU_W, LRU_W), lambda i, t: (0, 0, 0)), pl.BlockSpec((1, LRU_W), fix),
            pl.BlockSpec((1, LRU_W), fix),
        ],
        out_specs=[pl.BlockSpec((1, tt, LRU_W), seq), pl.BlockSpec((1, CONV_W - 1, LRU_W), per_b),
                   pl.BlockSpec((1, 1, LRU_W), per_b)],
        out_shape=[jax.ShapeDtypeStruct((b, s, LRU_W), F32), jax.ShapeDtypeStruct((b, CONV_W - 1, LRU_W), F32),
                   jax.ShapeDtypeStruct((b, 1, LRU_W), F32)],
        scratch_shapes=[pltpu.VMEM((tt + 8, LRU_W), F32), pltpu.VMEM((1, LRU_W), F32)],
        compiler_params=_cparams(2),
        name="lru_seq",
    )(ax, ag, buf0, h0.reshape(b, 1, LRU_W), cw, cb, wa, ba, wx, bx, lam)


def _lru_step_body(ax_ref, ag_ref, b0_ref, b1_ref, b2_ref, h0_ref, cw_ref, cb_ref, wa_ref, ba_ref, wx_ref, bx_ref,
                   lam_ref, y_ref, hn_ref):
    cw = cw_ref[...]
    xc = (cb_ref[...] + b0_ref[...] * cw[0:1, :] + b1_ref[...] * cw[1:2, :] + b2_ref[...] * cw[2:3, :]
          + ax_ref[...] * cw[3:4, :])
    a, b = _lru_gates(xc, wa_ref, ba_ref, wx_ref, bx_ref, lam_ref)
    hs = b + a * h0_ref[...]
    y_ref[...] = hs * jax.nn.gelu(ag_ref[...])
    hn_ref[...] = hs


def _lru_step(ax, ag, buf, h0, cw, cb, wa, ba, wx, bx, lam):
    n = ax.shape[0]
    return pl.pallas_call(
        _lru_step_body,
        out_shape=[jax.ShapeDtypeStruct((n, LRU_W), F32), jax.ShapeDtypeStruct((n, LRU_W), F32)],
        compiler_params=pltpu.CompilerParams(vmem_limit_bytes=VMEM_LIMIT_BYTES),
        name="lru_step",
    )(ax, ag, buf[:, 0], buf[:, 1], buf[:, 2], h0, cw, cb, wa, ba, wx, bx, lam)


def _group_norm_gate(o, gn, g):
    mu = jnp.mean(o, axis=-1, keepdims=True)
    var = jnp.mean(jnp.square(o - mu), axis=-1, keepdims=True)
    return jax.nn.silu(g) * ((o - mu) * lax.rsqrt(var + EPS) * gn)


def _ret_seq_body(q_ref, kt_ref, v_ref, g_ref, s0_ref, dmat_ref, rdec_ref, cdec_ref, gl_ref, gn_ref,
                  y_ref, sn_ref, state):
    c = pl.program_id(1)

    @pl.when(c == 0)
    def _():
        state[...] = s0_ref[0]

    q = q_ref[0]
    kt = kt_ref[0]
    v = v_ref[0]
    g = g_ref[0]
    for hh in range(RET_HEADS):
        qh = q[:, hh * RET_DK:(hh + 1) * RET_DK]
        kth = kt[hh * RET_DK:(hh + 1) * RET_DK, :]
        vh = _split(v[:, hh * RET_DV:(hh + 1) * RET_DV])
        s_old = state[hh]
        s = _dot3v(qh, kth) * dmat_ref[hh]
        o = _dot3(s, *vh) + _dot3v(qh, s_old) * rdec_ref[hh]
        state[hh] = gl_ref[hh] * s_old + _dot3(kth * cdec_ref[hh], *vh)
        sl = slice(hh * RET_DV, (hh + 1) * RET_DV)
        y_ref[0, :, sl] = _group_norm_gate(o, gn_ref[:, sl], g[:, sl])

    @pl.when(c == pl.num_programs(1) - 1)
    def _():
        sn_ref[0] = state[...]


def _ret_tables(chunk):
    logg = jnp.log1p(-jnp.exp2(-5.0 - jnp.arange(RET_HEADS, dtype=F32)))
    idx = jnp.arange(chunk, dtype=F32)
    diff = idx[:, None] - idx[None, :]
    dmat = jnp.where(diff >= 0, jnp.exp(jnp.maximum(diff, 0.0)[None] * logg[:, None, None]), 0.0)
    rdec = jnp.exp((idx + 1.0)[None, :, None] * logg[:, None, None])
    cdec = jnp.exp((chunk - 1.0 - idx)[None, None, :] * logg[:, None, None])
    gl = jnp.broadcast_to(jnp.exp(chunk * logg)[:, None, None], (RET_HEADS, RET_DK, RET_DV))
    return dmat, rdec, cdec, gl, logg


def _ret_seq(q, kt, v, g, s0, gn):
    b, s, _ = q.shape
    ch = RET_L
    dmat, rdec, cdec, gl, _ = _ret_tables(ch)
    seq = lambda i, c: (i, c, 0)
    fix3 = lambda i, c: (0, 0, 0)
    return pl.pallas_call(
        _ret_seq_body,
        grid=(b, s // ch),
        in_specs=[
            pl.BlockSpec((1, ch, RET_HEADS * RET_DK), seq),
            pl.BlockSpec((1, RET_HEADS * RET_DK, ch), lambda i, c: (i, 0, c)),
            pl.BlockSpec((1, ch, GROUP_W), seq), pl.BlockSpec((1, ch, GROUP_W), seq),
            pl.BlockSpec((1, RET_HEADS, RET_DK, RET_DV), lambda i, c: (i, 0, 0, 0)),
            pl.BlockSpec((RET_HEADS, ch, ch), fix3), pl.BlockSpec((RET_HEADS, ch, 1), fix3),
            pl.BlockSpec((RET_HEADS, 1, ch), fix3), pl.BlockSpec((RET_HEADS, RET_DK, RET_DV), fix3),
            pl.BlockSpec((1, GROUP_W), lambda i, c: (0, 0)),
        ],
        out_specs=[pl.BlockSpec((1, ch, GROUP_W), seq),
                   pl.BlockSpec((1, RET_HEADS, RET_DK, RET_DV), lambda i, c: (i, 0, 0, 0))],
        out_shape=[jax.ShapeDtypeStruct((b, s, GROUP_W), F32),
                   jax.ShapeDtypeStruct((b, RET_HEADS, RET_DK, RET_DV), F32)],
        scratch_shapes=[pltpu.VMEM((RET_HEADS, RET_DK, RET_DV), F32)],
        compiler_params=_cparams(2),
        name="ret_seq",
    )(q, kt, v, g, s0, dmat, rdec, cdec, gl, gn)


def _ret_step_body(q_ref, k_ref, v_ref, g_ref, s0_ref, dec_ref, gn_ref, y_ref, sn_ref):
    q = q_ref[...]
    k = k_ref[...]
    v = v_ref[...]
    s0 = s0_ref[...]
    dec = dec_ref[...]
    qk = jnp.sum(q * k, axis=1, keepdims=True)
    o = qk * v + jnp.sum(q * s0, axis=1, keepdims=True) * dec
    sn_ref[...] = dec * s0 + k * v
    y_ref[...] = _group_norm_gate(o, gn_ref[...], g_ref[...])


def _ret_step(q, k, v, g, s0, gn):
    n = q.shape[0]
    x = n * RET_HEADS
    xb = 64
    _, _, _, _, logg = _ret_tables(1)
    dec = jnp.broadcast_to(jnp.exp(logg)[None, :, None, None], (xb // RET_HEADS, RET_HEADS, 1, RET_DV))
    dec = dec.reshape(xb, 1, RET_DV)
    gn3 = jnp.broadcast_to(gn.reshape(1, RET_HEADS, 1, RET_DV), (xb // RET_HEADS, RET_HEADS, 1, RET_DV))
    gn3 = gn3.reshape(xb, 1, RET_DV)
    blk = lambda i: (i, 0, 0)
    fix = lambda i: (0, 0, 0)
    y, sn = pl.pallas_call(
        _ret_step_body,
        grid=(x // xb,),
        in_specs=[pl.BlockSpec((xb, RET_DK, 1), blk), pl.BlockSpec((xb, RET_DK, 1), blk),
                  pl.BlockSpec((xb, 1, RET_DV), blk), pl.BlockSpec((xb, 1, RET_DV), blk),
                  pl.BlockSpec((xb, RET_DK, RET_DV), blk),
                  pl.BlockSpec((xb, 1, RET_DV), fix), pl.BlockSpec((xb, 1, RET_DV), fix)],
        out_specs=[pl.BlockSpec((xb, 1, RET_DV), blk), pl.BlockSpec((xb, RET_DK, RET_DV), blk)],
        out_shape=[jax.ShapeDtypeStruct((x, 1, RET_DV), F32), jax.ShapeDtypeStruct((x, RET_DK, RET_DV), F32)],
        compiler_params=_cparams(1),
        name="ret_step",
    )(q.reshape(x, RET_DK, 1), k.reshape(x, RET_DK, 1), v.reshape(x, 1, RET_DV), g.reshape(x, 1, RET_DV),
      s0.reshape(x, RET_DK, RET_DV), dec, gn3)
    return y.reshape(n, GROUP_W), sn.reshape(n, RET_HEADS, RET_DK, RET_DV)


def _diff_lambda(lq1_ref, lk1_ref, lq2_ref, lk2_ref, lam_init):
    return (jnp.exp(jnp.sum(lq1_ref[...] * lk1_ref[...], axis=-1, keepdims=True))
            - jnp.exp(jnp.sum(lq2_ref[...] * lk2_ref[...], axis=-1, keepdims=True)) + lam_init)


def _flash_body(*refs, tq, tk, e, diff, lam_init):
    n_in = 8 if diff else 3
    q_ref, k_ref, vt_ref = refs[:3]
    o_ref = refs[n_in]
    scr = refs[n_in + 1:]
    n_blk = len(scr) // 2
    m_scrs, acc_scrs = scr[:n_blk], scr[n_blk:]
    if diff:
        lq1_ref, lk1_ref, lq2_ref, lk2_ref, sub_ref = refs[3:8]
    qi = pl.program_id(2)
    cols = q_ref.shape[3]
    cb_w = cols // n_blk
    n_full = (qi * tq) // tk
    for m_scr, acc_scr in zip(m_scrs, acc_scrs):
        m_scr[...] = jnp.full(m_scr.shape, -jnp.inf, F32)
        acc_scr[...] = jnp.zeros(acc_scr.shape, F32)

    def step(j, masked):
        kj = k_ref[0, 0, j]
        vj = vt_ref[0, 0, j]
        sts = [_dot_nt(kj, q_ref[0, 0, 0, cb * cb_w:(cb + 1) * cb_w, :]) for cb in range(n_blk)]
        for cb, (st, m_scr, acc_scr) in enumerate(zip(sts, m_scrs, acc_scrs)):
            if masked:
                kpos = j * tk + lax.broadcasted_iota(jnp.int32, st.shape, 0)
                qpos = qi * tq + ((cb * cb_w + lax.broadcasted_iota(jnp.int32, st.shape, 1)) & (tq - 1))
                st = jnp.where(kpos <= qpos, st, -jnp.inf)
            m_old = m_scr[...]
            m_new = jnp.maximum(m_old, jnp.max(st, axis=0, keepdims=True))
            p = jnp.exp2(st - m_new).astype(BF16)
            acc_scr[...] = jnp.exp2(m_old - m_new) * acc_scr[...] + _dot(vj, p)
            m_scr[...] = m_new

    def full_step(j, carry):
        step(j, False)
        return carry

    lax.fori_loop(0, n_full, full_step, 0)
    step(n_full, True)
    acc = jnp.concatenate([a[...] for a in acc_scrs], axis=1) if n_blk > 1 else acc_scrs[0][...]
    o = acc[0:e, :] / acc[e:e + 1, :]
    if diff:
        lam = _diff_lambda(lq1_ref, lk1_ref, lq2_ref, lk2_ref, lam_init)
        half = cols // 2
        a = o[:, 0:half] - lam * o[:, half:cols]
        ms = jnp.mean(a * a, axis=0, keepdims=True)
        y = a * lax.rsqrt(ms + EPS) * sub_ref[...] * (1.0 - lam_init)
        o_ref[...] = jnp.concatenate([y[:, r * tq:(r + 1) * tq] for r in range(half // tq)], axis=0).T
    else:
        for hh in range(cols // tq):
            o_ref[:, hh * e:(hh + 1) * e] = o[:, hh * tq:(hh + 1) * tq].T


def _flash(q, k, vt, extra=None, lam_init=0.0):
    b, hk, nq, cols, dk = q.shape
    nk, ea = k.shape[2], vt.shape[3]
    e = ea - BF16_ROWS
    diff = extra is not None
    out_w = (cols // 2 if diff else cols) // FLASH_TQ * e
    in_specs = [
        pl.BlockSpec((1, 1, 1, cols, dk), lambda i, h, j: (i, h, j, 0, 0)),
        pl.BlockSpec((1, 1, nk, FLASH_TK, dk), lambda i, h, j: (i, h, 0, 0, 0)),
        pl.BlockSpec((1, 1, nk, ea, FLASH_TK), lambda i, h, j: (i, h, 0, 0, 0)),
    ]
    args = [q, k, vt]
    if diff:
        in_specs += [pl.BlockSpec((1, DIFF_D), lambda i, h, j: (0, 0))] * 4
        in_specs += [pl.BlockSpec((DIFF_V, 1), lambda i, h, j: (0, 0))]
        args += list(extra)
    return pl.pallas_call(
        functools.partial(_flash_body, tq=FLASH_TQ, tk=FLASH_TK, e=e, diff=diff, lam_init=lam_init),
        grid=(b, hk, nq),
        in_specs=in_specs,
        out_specs=pl.BlockSpec((FLASH_TQ, out_w), lambda i, h, j: (i * nq + j, h)),
        out_shape=jax.ShapeDtypeStruct((b * nq * FLASH_TQ, hk * out_w), F32),
        scratch_shapes=([pltpu.VMEM((1, FLASH_CB), F32)] * (cols // FLASH_CB)
                        + [pltpu.VMEM((ea, FLASH_CB), F32)] * (cols // FLASH_CB)),
        compiler_params=_cparams(3),
        name="flash_diff" if diff else "flash_mla",
    )(*args)


def _page_copies(pt_ref, n, slot, srcs, bufs, sems, n_pages, pick):
    copies = []
    for j in range(n_pages):
        page = pt_ref[n, j]
        for src, buf, sem, row_major in zip(srcs, bufs, sems, (pick == "mla", False)):
            if row_major:
                dst = buf.at[slot, pl.ds(j * PAGE, PAGE), :]
            else:
                dst = buf.at[slot, :, pl.ds(j * PAGE, PAGE)]
            copies.append(pltpu.make_async_copy(src(page), dst, sem.at[slot]))
    return copies


def _mla_decode_body(pt_ref, ql_ref, qp_ref, cn_ref, kn_ref, ckv_hbm, kpet_hbm, o_ref, cbuf, kbuf, c16, sem_c, sem_k,
                     *, layer, n_pages):
    n = pl.program_id(0)
    slot = n % 2
    srcs = (lambda pg: ckv_hbm.at[layer, pg], lambda pg: kpet_hbm.at[layer, pg])

    def copies(nn, sl):
        return _page_copies(pt_ref, nn, sl, srcs, (cbuf, kbuf), (sem_c, sem_k), n_pages, "mla")

    @pl.when(n == 0)
    def _():
        for cp in copies(0, 0):
            cp.start()

    @pl.when(n + 1 < pl.num_programs(0))
    def _():
        for cp in copies(n + 1, 1 - slot):
            cp.start()

    for cp in copies(n, slot):
        cp.wait()

    ql = ql_ref[0]
    qp = qp_ref[0]
    cn = cn_ref[0]
    kn = kn_ref[0]
    qlb, qpb = ql.astype(BF16), qp.astype(BF16)
    s_own = (jnp.sum(ql * cn, axis=-1, keepdims=True) + jnp.sum(qp * kn, axis=-1, keepdims=True)) * MLA_SCALE
    n_chunks = (n_pages * PAGE) // DECODE_CHUNK
    scores = []
    for c in range(n_chunks):
        lo = c * DECODE_CHUNK
        c16[lo:lo + DECODE_CHUNK, :] = cbuf[slot, lo:lo + DECODE_CHUNK, :].astype(BF16)
        s = _dot_nt(qlb, c16[lo:lo + DECODE_CHUNK, :]) + _dot(qpb, kbuf[slot, :, lo:lo + DECODE_CHUNK].astype(BF16))
        scores.append(s * MLA_SCALE)
    m = s_own
    for s in scores:
        m = jnp.maximum(m, jnp.max(s, axis=-1, keepdims=True))
    p_own = jnp.exp(s_own - m)
    den = p_own
    acc = p_own * cn
    for c, s in enumerate(scores):
        lo = c * DECODE_CHUNK
        p = jnp.exp(s - m)
        den = den + jnp.sum(p, axis=-1, keepdims=True)
        acc = acc + _dot(p.astype(BF16), c16[lo:lo + DECODE_CHUNK, :])
    o_ref[0] = acc / den


def _mla_decode(page_table, q_lat, q_pe, ckv_new, kpe_new, cache_ckv, cache_kpet, layer):
    n, n_pages = page_table.shape
    keys = n_pages * PAGE
    blk = lambda i, pt: (i, 0, 0)
    grid_spec = pltpu.PrefetchScalarGridSpec(
        num_scalar_prefetch=1,
        grid=(n,),
        in_specs=[pl.BlockSpec((1, MLA_HEADS, MLA_KV_RANK), blk), pl.BlockSpec((1, MLA_HEADS, MLA_ROPE), blk),
                  pl.BlockSpec((1, 1, MLA_KV_RANK), blk), pl.BlockSpec((1, 1, MLA_ROPE), blk),
                  pl.BlockSpec(memory_space=pl.ANY), pl.BlockSpec(memory_space=pl.ANY)],
        out_specs=pl.BlockSpec((1, MLA_HEADS, MLA_KV_RANK), blk),
        scratch_shapes=[pltpu.VMEM((2, keys, MLA_KV_RANK), F32), pltpu.VMEM((2, MLA_ROPE, keys), F32),
                        pltpu.VMEM((keys, MLA_KV_RANK), BF16),
                        pltpu.SemaphoreType.DMA((2,)), pltpu.SemaphoreType.DMA((2,))],
    )
    return pl.pallas_call(
        functools.partial(_mla_decode_body, layer=layer, n_pages=n_pages),
        grid_spec=grid_spec,
        out_shape=jax.ShapeDtypeStruct((n, MLA_HEADS, MLA_KV_RANK), F32),
        compiler_params=_cparams(1),
        name="mla_decode",
    )(page_table, q_lat.reshape(n, MLA_HEADS, MLA_KV_RANK), q_pe.reshape(n, MLA_HEADS, MLA_ROPE),
      ckv_new.reshape(n, 1, MLA_KV_RANK), kpe_new.reshape(n, 1, MLA_ROPE), cache_ckv, cache_kpet)


def _diff_decode_body(pt_ref, q_ref, kn_ref, vn_ref, lq1_ref, lk1_ref, lq2_ref, lk2_ref, sub_ref, kt_hbm, vt_hbm,
                      o_ref, kbuf, vbuf, sem_k, sem_v, *, layer, n_pages, lam_init):
    n = pl.program_id(0)
    g = pl.program_id(1)
    step = n * DIFF_KV_HEADS + g
    slot = step % 2
    srcs_of = lambda gg: (lambda pg: kt_hbm.at[layer, pg, gg], lambda pg: vt_hbm.at[layer, pg, gg])

    def copies(nn, gg, sl):
        return _page_copies(pt_ref, nn, sl, srcs_of(gg), (kbuf, vbuf), (sem_k, sem_v), n_pages, "diff")

    @pl.when(step == 0)
    def _():
        for cp in copies(0, 0, 0):
            cp.start()

    @pl.when(step + 1 < pl.num_programs(0) * DIFF_KV_HEADS)
    def _():
        nxt = step + 1
        for cp in copies(nxt // DIFF_KV_HEADS, nxt % DIFF_KV_HEADS, 1 - slot):
            cp.start()

    for cp in copies(n, g, slot):
        cp.wait()

    q = q_ref[0, 0]
    kn = kn_ref[0, 0]
    vn = vn_ref[0, 0]
    qb = q.astype(BF16)
    s_own = jnp.sum(q * kn, axis=-1, keepdims=True) * DIFF_SCALE
    n_chunks = (n_pages * PAGE) // DECODE_CHUNK
    scores = []
    for c in range(n_chunks):
        lo = c * DECODE_CHUNK
        scores.append(_dot(qb, kbuf[slot, :, lo:lo + DECODE_CHUNK].astype(BF16)) * DIFF_SCALE)
    m = s_own
    for s in scores:
        m = jnp.maximum(m, jnp.max(s, axis=-1, keepdims=True))
    p_own = jnp.exp(s_own - m)
    den = p_own
    acc = p_own * vn
    for c, s in enumerate(scores):
        lo = c * DECODE_CHUNK
        p = jnp.exp(s - m)
        den = den + jnp.sum(p, axis=-1, keepdims=True)
        acc = acc + _dot_nt(p.astype(BF16), vbuf[slot, :, lo:lo + DECODE_CHUNK].astype(BF16))
    o = acc / den
    lam = _diff_lambda(lq1_ref, lk1_ref, lq2_ref, lk2_ref, lam_init)
    for r in range(DIFF_REP):
        a = o[2 * r:2 * r + 1, :] - lam * o[2 * r + 1:2 * r + 2, :]
        o_ref[0, 0, r:r + 1, :] = _rms(a, sub_ref[...]) * (1.0 - lam_init)


def _diff_decode(page_table, qbd, k_new, v_new, lam_refs, subln, cache_kt, cache_vt, layer, lam_init):
    n, n_pages = page_table.shape
    keys = n_pages * PAGE
    blk = lambda i, g, pt: (i, g, 0, 0)
    fix = lambda i, g, pt: (0, 0)
    grid_spec = pltpu.PrefetchScalarGridSpec(
        num_scalar_prefetch=1,
        grid=(n, DIFF_KV_HEADS),
        in_specs=[pl.BlockSpec((1, 1, 2 * DIFF_REP, 2 * DIFF_D), blk), pl.BlockSpec((1, 1, 1, 2 * DIFF_D), blk),
                  pl.BlockSpec((1, 1, 1, DIFF_V), blk)]
        + [pl.BlockSpec((1, DIFF_D), fix)] * 4 + [pl.BlockSpec((1, DIFF_V), fix)]
        + [pl.BlockSpec(memory_space=pl.ANY), pl.BlockSpec(memory_space=pl.ANY)],
        out_specs=pl.BlockSpec((1, 1, DIFF_REP, DIFF_V), blk),
        scratch_shapes=[pltpu.VMEM((2, 2 * DIFF_D, keys), F32), pltpu.VMEM((2, DIFF_V, keys), F32),
                        pltpu.SemaphoreType.DMA((2,)), pltpu.SemaphoreType.DMA((2,))],
    )
    return pl.pallas_call(
        functools.partial(_diff_decode_body, layer=layer, n_pages=n_pages, lam_init=lam_init),
        grid_spec=grid_spec,
        out_shape=jax.ShapeDtypeStruct((n, DIFF_KV_HEADS, DIFF_REP, DIFF_V), F32),
        compiler_params=_cparams(2),
        name="diff_decode",
    )(page_table, qbd, k_new, v_new, *lam_refs, subln, cache_kt, cache_vt)


def _post_body(x_ref, ya_ref, yb_ref, ol_ref, yd_ref, wuv_ref, wout_ref, n2_ref, wr_ref, br_ref,
               x1_ref, hm_ref, cmb_ref):
    ol = ol_ref[...]
    yc = jnp.concatenate(
        [_dot3(ol[:, hh * MLA_KV_RANK:(hh + 1) * MLA_KV_RANK], wuv_ref[0, hh], wuv_ref[1, hh])
         for hh in range(MLA_HEADS)], axis=1)
    y = None
    for gi, yg in enumerate((ya_ref[...], yb_ref[...], yc, yd_ref[...])):
        rows = slice(gi * GROUP_W, (gi + 1) * GROUP_W)
        part = _dot3(yg, wout_ref[0, rows, :], wout_ref[1, rows, :])
        y = part if y is None else y + part
    x1 = x_ref[...] + y
    x1_ref[...] = x1
    hm = _rms(x1, n2_ref[...])
    hm_ref[...] = hm
    logits = _dot3(hm, wr_ref[0], wr_ref[1]) + br_ref[...]
    lane_i = lax.broadcasted_iota(jnp.int32, logits.shape, 1)
    lane = lane_i.astype(F32)
    far = float(LANES)
    gl = jnp.where((lane_i >= MOE_EXPERTS) & (lane_i < MOE_EXPERTS + MOE_GROUPS), logits, -jnp.inf)
    gmax = jnp.max(gl, axis=1, keepdims=True)
    gsel = jnp.min(jnp.where(gl == gmax, lane, far), axis=1, keepdims=True) - float(MOE_EXPERTS)
    gw = 1.0 / jnp.sum(jnp.exp(gl - gmax), axis=1, keepdims=True)
    lane_group = (lane_i >> 2).astype(F32)
    el = jnp.where((lane_i < MOE_EXPERTS) & (lane_group == gsel), logits, -jnp.inf)
    v1 = jnp.max(el, axis=1, keepdims=True)
    i1 = jnp.min(jnp.where(el == v1, lane, far), axis=1, keepdims=True)
    el2 = jnp.where(lane == i1, -jnp.inf, el)
    v2 = jnp.max(el2, axis=1, keepdims=True)
    i2 = jnp.min(jnp.where(el2 == v2, lane, far), axis=1, keepdims=True)
    e21 = jnp.exp(v2 - v1)
    w1 = 1.0 / (1.0 + e21)
    w2 = e21 / (1.0 + e21)
    cmb_ref[...] = jnp.where(lane == i1, w1 * gw, jnp.where(lane == i2, w2 * gw, 0.0))


def _post(x2d, ya, yb, ol, yd, wuv, wout, n2, wr, br, tm):
    m = x2d.shape[0]
    row = lambda i: (i, 0)
    fix = lambda i: (0, 0)
    return pl.pallas_call(
        _post_body,
        grid=(m // tm,),
        in_specs=[pl.BlockSpec((tm, D_MODEL), row), pl.BlockSpec((tm, GROUP_W), row), pl.BlockSpec((tm, GROUP_W), row),
                  pl.BlockSpec((tm, MLA_HEADS * MLA_KV_RANK), row), pl.BlockSpec((tm, GROUP_W), row),
                  pl.BlockSpec((2, MLA_HEADS, MLA_KV_RANK, MLA_V), lambda i: (0, 0, 0, 0)),
                  pl.BlockSpec((2, D_MODEL, D_MODEL), lambda i: (0, 0, 0)), pl.BlockSpec((1, D_MODEL), fix),
                  pl.BlockSpec((2, D_MODEL, LANES), lambda i: (0, 0, 0)), pl.BlockSpec((1, LANES), fix)],
        out_specs=[pl.BlockSpec((tm, D_MODEL), row), pl.BlockSpec((tm, D_MODEL), row), pl.BlockSpec((tm, LANES), row)],
        out_shape=[jax.ShapeDtypeStruct((m, D_MODEL), F32), jax.ShapeDtypeStruct((m, D_MODEL), F32),
                   jax.ShapeDtypeStruct((m, LANES), F32)],
        compiler_params=_cparams(1),
        name="post",
    )(x2d, ya, yb, ol, yd, wuv, wout, n2, wr, br)


def _moe_body(hm_ref, cmb_ref, x1_ref, wg_ref, wu_ref, wd_ref, fn_ref, o_ref, *, final, precise):
    e = pl.program_id(1)

    @pl.when(e == 0)
    def _():
        o_ref[...] = x1_ref[...]

    if precise:
        mm = lambda a, w_ref: _dot3(a, w_ref[0, 0], w_ref[1, 0])
    else:
        mm = lambda a, w_ref: _dot(a.astype(BF16), w_ref[0, 0])
    h = hm_ref[...]
    hg = mm(h, wg_ref)
    hu = mm(h, wu_ref)
    cmb = cmb_ref[...]
    lane = lax.broadcasted_iota(jnp.int32, cmb.shape, 1)
    c = jnp.sum(jnp.where(lane == e, cmb, 0.0), axis=1, keepdims=True)
    act = jax.nn.silu(hg) * hu * c
    o_ref[...] += mm(act, wd_ref)
    if final:
        @pl.when(e == pl.num_programs(1) - 1)
        def _():
            o_ref[...] = _rms(o_ref[...], fn_ref[...])


def _moe(hm, cmb, x1, wg, wu, wd, fn, tm, final, precise):
    m = hm.shape[0]
    halves = 2 if precise else 1
    row = lambda i, e: (i, 0)
    wsel = lambda i, e: (0, e, 0, 0)
    return pl.pallas_call(
        functools.partial(_moe_body, final=final, precise=precise),
        grid=(m // tm, MOE_EXPERTS),
        in_specs=[pl.BlockSpec((tm, D_MODEL), row), pl.BlockSpec((tm, LANES), row), pl.BlockSpec((tm, D_MODEL), row),
                  pl.BlockSpec((halves, 1, D_MODEL, MOE_FF), wsel), pl.BlockSpec((halves, 1, D_MODEL, MOE_FF), wsel),
                  pl.BlockSpec((halves, 1, MOE_FF, D_MODEL), wsel), pl.BlockSpec((1, D_MODEL), lambda i, e: (0, 0))],
        out_specs=pl.BlockSpec((tm, D_MODEL), row),
        out_shape=jax.ShapeDtypeStruct((m, D_MODEL), F32),
        compiler_params=_cparams(2),
        name="moe",
    )(hm, cmb, x1, wg, wu, wd, fn)


def _rope_tables(pos):
    half = MLA_ROPE // 2
    inv = ROPE_THETA ** (-jnp.arange(half, dtype=F32) / half)
    ang = pos.astype(F32)[:, None] * inv[None, :]
    c, s = jnp.cos(ang), jnp.sin(ang)
    return (jnp.tile(jnp.concatenate([c, c], axis=-1), (1, 4)),
            jnp.tile(jnp.concatenate([-s, s], axis=-1), (1, 4)))


def _pad_cols(w, sizes, pads):
    pieces, off = [], 0
    for sz, pd in zip(sizes, pads):
        piece = w[:, off:off + sz]
        if pd > sz:
            piece = jnp.pad(piece, ((0, 0), (0, pd - sz)))
        pieces.append(piece)
        off += sz
    return jnp.concatenate(pieces, axis=1)


def _block_diag(w):
    nb, bw, _ = w.shape
    out = jnp.zeros((nb * bw, nb * bw), w.dtype)
    for i in range(nb):
        out = out.at[i * bw:(i + 1) * bw, i * bw:(i + 1) * bw].set(w[i])
    return out


def _hilo(w):
    hi = w.astype(BF16)
    return jnp.stack([hi, (w - hi.astype(F32)).astype(BF16)])


def _layer_weights(l, P):
    w = {}
    w["n1"] = P["norm1"][l][None, :]
    w["w_in"] = _hilo(_pad_cols(P["w_in"][l], _IN_SIZES, _IN_PAD))
    w["cw"] = P["conv_w"][l]
    w["cb"] = P["conv_b"][l][None, :]
    w["wa"] = _hilo(_block_diag(P["lru_wa"][l]))
    w["ba"] = P["lru_ba"][l][None, :]
    w["wx"] = _hilo(_block_diag(P["lru_wx"][l]))
    w["bx"] = P["lru_bx"][l][None, :]
    w["lam"] = P["lru_lambda"][l][None, :]
    w["gn"] = P["ret_gn"][l][None, :]
    w["qn"] = jnp.pad(P["mla_q_norm"][l], (0, 256 - MLA_Q_RANK))[None, :]
    wuq = P["mla_w_uq"][l].reshape(MLA_Q_RANK, MLA_HEADS, MLA_NOPE + MLA_ROPE)
    pad_rows = ((0, 256 - MLA_Q_RANK), (0, 0))
    w["wuqn"] = _hilo(jnp.pad(wuq[:, :, :MLA_NOPE].reshape(MLA_Q_RANK, MLA_HEADS * MLA_NOPE), pad_rows))
    w["wuqr"] = _hilo(jnp.pad(wuq[:, :, MLA_NOPE:].reshape(MLA_Q_RANK, MLA_HEADS * MLA_ROPE), pad_rows))
    w["wukt"] = _hilo(jnp.transpose(P["mla_w_uk"][l], (1, 2, 0)))
    w["kvn"] = P["mla_kv_norm"][l][None, :]
    w["wuv"] = _hilo(jnp.transpose(P["mla_w_uv"][l], (1, 0, 2)))
    w["lam_refs"] = tuple(P[k][l][None, :] for k in ("diff_lq1", "diff_lk1", "diff_lq2", "diff_lk2"))
    w["subln"] = P["diff_subln"][l]
    w["w_out"] = _hilo(P["w_out"][l])
    w["n2"] = P["norm2"][l][None, :]
    wr = jnp.concatenate([P["moe_w_re"][l], P["moe_w_rg"][l]], axis=1)
    w["wr"] = _hilo(jnp.pad(wr, ((0, 0), (0, LANES - wr.shape[1]))))
    br = jnp.concatenate([P["moe_b_re"][l], P["moe_b_rg"][l]])
    w["br"] = jnp.pad(br, (0, LANES - br.shape[0]))[None, :]
    w["wg"] = _hilo(P["moe_w_gate"][l])
    w["wu"] = _hilo(P["moe_w_up"][l])
    w["wd"] = _hilo(P["moe_w_down"][l])
    return w


def _prompt_mixers(l, w, pieces, b, s, lam_init):
    (lrux, lrug, rq, rkt, rv, rg, ckv, kpe, dk, dv, qm, km, vtm, qd, kd, vtd) = pieces
    tq, tk = FLASH_TQ, FLASH_TK
    nq, nk = s // tq, s // tk
    r3 = lambda a: a.reshape(b, s, a.shape[-1])
    zeros = lambda *sh: jnp.zeros(sh, F32)
    ya, nbuf, hlast = _lru_seq(r3(lrux), r3(lrug), zeros(b, CONV_W - 1, LRU_W), zeros(b, LRU_W), w["cw"], w["cb"],
                               w["wa"], w["ba"], w["wx"], w["bx"], w["lam"])
    yb, s_new = _ret_seq(r3(rq), rkt, r3(rv), r3(rg), zeros(b, RET_HEADS, RET_DK, RET_DV), w["gn"])
    ol = _flash(qm.reshape(b, 1, nq, MLA_HEADS * tq, qm.shape[-1]), km.reshape(b, 1, nk, tk, km.shape[-1]),
                vtm.reshape(b, 1, nk, vtm.shape[1], tk))
    yd = _flash(qd.reshape(b, DIFF_KV_HEADS, nq, 2 * DIFF_REP * tq, qd.shape[-1]), kd, vtd,
                extra=w["lam_refs"] + (w["subln"][:, None],), lam_init=lam_init)
    new = (nbuf, hlast.reshape(b, LRU_W), s_new, r3(ckv), r3(kpe),
           dk.reshape(b, s, DIFF_KV_HEADS, 2 * DIFF_D), dv.reshape(b, s, DIFF_KV_HEADS, DIFF_V))
    return ya.reshape(b * s, LRU_W), yb.reshape(b * s, GROUP_W), ol, yd, new


def _sample_mixers(l, w, pieces, n, lru_buf, lru_h, ret_s, caches, page_table, lam_init):
    (lrux, lrug, rq, rk, rv, rg, qlat, qpe, ckv, kpe, dq, dk, dv) = pieces
    cache_ckv, cache_kpet, cache_kt, cache_vt = caches
    ya, h_new = _lru_step(lrux, lrug, lru_buf, lru_h, w["cw"], w["cb"], w["wa"], w["ba"], w["wx"], w["bx"], w["lam"])
    nbuf = jnp.concatenate([lru_buf[:, 1:], lrux[:, None, :]], axis=1)
    yb, s_new = _ret_step(rq, rk, rv, rg, ret_s, w["gn"])
    ol = _mla_decode(page_table, qlat, qpe, ckv, kpe, cache_ckv, cache_kpet, l).reshape(n, MLA_HEADS * MLA_KV_RANK)
    q5 = dq.reshape(n, DIFF_KV_HEADS, DIFF_REP, 2, DIFF_D)
    zq = jnp.zeros_like(q5[..., 0, :])
    qbd = jnp.stack([jnp.concatenate([q5[..., 0, :], zq], -1), jnp.concatenate([zq, q5[..., 1, :]], -1)], axis=3)
    qbd = qbd.reshape(n, DIFF_KV_HEADS, 2 * DIFF_REP, 2 * DIFF_D)
    yd = _diff_decode(page_table, qbd, dk.reshape(n, DIFF_KV_HEADS, 1, 2 * DIFF_D),
                      dv.reshape(n, DIFF_KV_HEADS, 1, DIFF_V), w["lam_refs"], w["subln"][None, :],
                      cache_kt, cache_vt, l, lam_init).reshape(n, GROUP_W)
    new = (nbuf, h_new, s_new, ckv.reshape(n, 1, MLA_KV_RANK), kpe.reshape(n, 1, MLA_ROPE),
           dk.reshape(n, 1, DIFF_KV_HEADS, 2 * DIFF_D), dv.reshape(n, 1, DIFF_KV_HEADS, DIFF_V))
    return ya, yb, ol, yd, new


def kernel(x_prompt, x_sample, state_lru_conv, state_lru_h, state_ret, cache_mla_ckv, cache_mla_kpe, cache_diff_k, cache_diff_v, page_table, norm1, w_in, conv_w, conv_b, lru_wa, lru_ba, lru_wx, lru_bx, lru_lambda, ret_gn, mla_q_norm, mla_w_uq, mla_kv_norm, mla_w_uk, mla_w_uv, diff_lq1, diff_lk1, diff_lq2, diff_lk2, diff_subln, w_out, norm2, moe_w_rg, moe_b_rg, moe_w_re, moe_b_re, moe_w_gate, moe_w_up, moe_w_down, final_norm):
    P = dict(norm1=norm1, w_in=w_in, conv_w=conv_w, conv_b=conv_b, lru_wa=lru_wa, lru_ba=lru_ba,
             lru_wx=lru_wx, lru_bx=lru_bx, lru_lambda=lru_lambda, ret_gn=ret_gn, mla_q_norm=mla_q_norm,
             mla_w_uq=mla_w_uq, mla_kv_norm=mla_kv_norm, mla_w_uk=mla_w_uk, mla_w_uv=mla_w_uv,
             diff_lq1=diff_lq1, diff_lk1=diff_lk1, diff_lq2=diff_lq2, diff_lk2=diff_lk2,
             diff_subln=diff_subln, w_out=w_out, norm2=norm2, moe_w_rg=moe_w_rg, moe_b_rg=moe_b_rg,
             moe_w_re=moe_w_re, moe_b_re=moe_b_re, moe_w_gate=moe_w_gate, moe_w_up=moe_w_up,
             moe_w_down=moe_w_down)
    depth = norm1.shape[0]
    b, s, _ = x_prompt.shape
    n = x_sample.shape[0]
    past_len = page_table.shape[1] * PAGE
    cos_p, sin_p = _rope_tables(jnp.arange(s, dtype=jnp.int32))
    cos_s, sin_s = _rope_tables(jnp.full((1,), past_len, jnp.int32))
    cos_s, sin_s = jnp.broadcast_to(cos_s, (n, LANES)), jnp.broadcast_to(sin_s, (n, LANES))
    caches = (cache_mla_ckv, jnp.swapaxes(cache_mla_kpe, 2, 3),
              jnp.transpose(cache_diff_k, (0, 1, 3, 4, 2)), jnp.transpose(cache_diff_v, (0, 1, 3, 4, 2)))
    fn = final_norm[None, :]
    tm_p = 512
    tm_in = 256
    xp = x_prompt.reshape(b * s, D_MODEL)
    xs = x_sample.reshape(n, D_MODEL)
    p_new, s_new = [], []
    for l in range(depth):
        w = _layer_weights(l, P)
        lam_init = 0.8 - 0.6 * math.exp(-0.3 * l)
        final = l == depth - 1
        proj = lambda x2d, cos, sin, tm, seq=None: _in_proj(x2d, w["n1"], w["w_in"], cos, sin, w["qn"], w["wuqn"],
                                                            w["wuqr"], w["wukt"], w["kvn"], tm, seq)
        ya, yb, ol, yd, st_p = _prompt_mixers(l, w, proj(xp, cos_p, sin_p, tm_in, s), b, s, lam_init)
        x1, hm, cmb = _post(xp, ya, yb, ol, yd, w["wuv"], w["w_out"], w["n2"], w["wr"], w["br"], tm_p)
        xp = _moe(hm, cmb, x1, w["wg"], w["wu"], w["wd"], fn, 1024, final, False)
        ya, yb, ol, yd, st_s = _sample_mixers(l, w, proj(xs, cos_s, sin_s, n), n, state_lru_conv[l], state_lru_h[l],
                                              state_ret[l], caches, page_table, lam_init)
        x1, hm, cmb = _post(xs, ya, yb, ol, yd, w["wuv"], w["w_out"], w["n2"], w["wr"], w["br"], n)
        xs = _moe(hm, cmb, x1, w["wg"], w["wu"], w["wd"], fn, n, final, True)
        p_new.append(st_p)
        s_new.append(st_s)
    outs_p = [jnp.stack([st[i] for st in p_new]) for i in range(7)]
    outs_s = [jnp.stack([st[i] for st in s_new]) for i in range(7)]
    return (xp.reshape(b, s, D_MODEL), xs.reshape(n, 1, D_MODEL), *outs_p, *outs_s)
```

```python
import functools
import math

import jax
import jax.numpy as jnp
from jax import lax
from jax.experimental import pallas as pl
from jax.experimental.pallas import tpu as pltpu

F32 = jnp.float32
BF16 = jnp.bfloat16

D_MODEL = 1024
PAGE = 128
GROUP_W = 256
LRU_W = 256
LRU_BLOCKS = 4
LRU_BW = 64
CONV_W = 4
LRU_C = 8.0
RET_HEADS = 4
RET_DV = 64
RET_DK = 32
MLA_HEADS = 4
MLA_V = 64
MLA_NOPE = 64
MLA_ROPE = 32
MLA_Q_RANK = 192
MLA_KV_RANK = 128
MLA_SCALE = (MLA_NOPE + MLA_ROPE) ** -0.5
DIFF_KV_HEADS = 2
DIFF_REP = 2
DIFF_V = 64
DIFF_D = 32
DIFF_SCALE = DIFF_D ** -0.5
MOE_GROUPS = 4
MOE_PER_GROUP = 4
MOE_EXPERTS = 16
MOE_FF = 256
ROPE_THETA = 10000.0
EPS = 1e-6

VMEM_LIMIT_BYTES = 56 * 1024 * 1024
LANES = 128

_IN_SIZES = (256, 256, 128, 128, 256, 256, 192, 128, 32, 256, 128, 128)
_IN_PAD = (256, 256, 128, 128, 256, 256, 256, 128, 128, 256, 128, 128)
_IN_WIDTH = sum(_IN_PAD)

RET_L = 512
LRU_TT = 512
FLASH_TQ = 512
FLASH_TK = 512
FLASH_CB = 256
BF16_ROWS = 16
LOG2E = math.log2(math.e)
DECODE_CHUNK = 2048


def _cparams(n_axes):
    return pltpu.CompilerParams(dimension_semantics=("arbitrary",) * n_axes,
                                vmem_limit_bytes=VMEM_LIMIT_BYTES)


def _dot(a, b):
    return jnp.dot(a, b, preferred_element_type=F32)


def _dot_nt(a, b):
    return lax.dot_general(a, b, (((1,), (1,)), ((), ())), preferred_element_type=F32)


def _split(a):
    hi = a.astype(BF16)
    return hi, (a - hi.astype(F32)).astype(BF16)


def _dot3(a, b_hi, b_lo):
    a_hi, a_lo = _split(a)
    return _dot(a_hi, b_hi) + (_dot(a_hi, b_lo) + _dot(a_lo, b_hi))


def _dot3v(a, b):
    return _dot3(a, *_split(b))


def _rms(x, g, width=None):
    w = x.shape[-1] if width is None else width
    ms = jnp.sum(x * x, axis=-1, keepdims=True) * (1.0 / w)
    return x * lax.rsqrt(ms + EPS) * g


def _ones_then_zero_rows(width):
    row = lax.broadcasted_iota(jnp.int32, (BF16_ROWS, width), 0)
    return jnp.where(row == 0, 1.0, 0.0).astype(BF16)


def _in_proj_body(x_ref, n1_ref, win_ref, cos_ref, sin_ref, qn_ref, wuqn_ref, wuqr_ref, wukt_ref, kvn_ref,
                  *out_refs, prompt):
    if prompt:
        (lrux_ref, lrug_ref, rq_ref, rkt_ref, rv_ref, rg_ref, ckv_ref, kpe_ref, dk_ref, dv_ref,
         qm_ref, km_ref, vtm_ref, qd_ref, kd_ref, vtd_ref) = out_refs
    else:
        (lrux_ref, lrug_ref, rq_ref, rk_ref, rv_ref, rg_ref, qlat_ref, qpe_ref, ckv_ref, kpe_ref,
         dq_ref, dk_ref, dv_ref) = out_refs
    x = x_ref[...]
    h = _rms(x, n1_ref[...])
    z = _dot3(h, win_ref[0], win_ref[1])
    cos = cos_ref[...]
    sin = sin_ref[...]
    lane = lax.broadcasted_iota(jnp.int32, cos.shape, 1)
    first_half = (lane & 31) < 16

    def rope(v):
        swapped = jnp.where(first_half, pltpu.roll(v, LANES - 16, 1), pltpu.roll(v, 16, 1))
        return v * cos + swapped * sin

    tm = x.shape[0]
    lrux_ref[...] = z[:, 0:256]
    lrug_ref[...] = z[:, 256:512]
    rq_ref[...] = rope(z[:, 512:640])
    rk = rope(z[:, 640:768]) * (RET_DK ** -0.5)
    rv_ref[...] = z[:, 768:1024]
    rg_ref[...] = z[:, 1024:1280]
    cq = _rms(z[:, 1280:1536], qn_ref[...], MLA_Q_RANK)
    q_nope = _dot3(cq, wuqn_ref[0], wuqn_ref[1])
    qpe = rope(_dot3(cq, wuqr_ref[0], wuqr_ref[1]))
    qlat = [_dot3(q_nope[:, hh * MLA_NOPE:(hh + 1) * MLA_NOPE], wukt_ref[0, hh], wukt_ref[1, hh])
            for hh in range(MLA_HEADS)]
    ckv = _rms(z[:, 1536:1664], kvn_ref[...])
    kpe_full = rope(z[:, 1664:1792])
    dq, dk, dv = z[:, 1792:2048], z[:, 2048:2176], z[:, 2176:2304]
    ckv_ref[...] = ckv
    kpe_ref[...] = kpe_full[:, 0:MLA_ROPE]
    dk_ref[...] = dk
    dv_ref[...] = dv
    if not prompt:
        rk_ref[...] = rk
        qpe_ref[...] = qpe
        for hh in range(MLA_HEADS):
            qlat_ref[:, hh * MLA_KV_RANK:(hh + 1) * MLA_KV_RANK] = qlat[hh]
        dq_ref[...] = dq
        return
    rkt_ref[0] = rk.T
    zeros = lambda w: jnp.zeros((tm, w), F32)
    c_mla = MLA_SCALE * LOG2E
    for hh in range(MLA_HEADS):
        pe = qpe[:, hh * MLA_ROPE:(hh + 1) * MLA_ROPE] * c_mla
        qm_ref[0, hh, :, 0:LANES] = (qlat[hh] * c_mla).astype(BF16)
        qm_ref[0, hh, :, LANES:2 * LANES] = jnp.concatenate([pe, zeros(LANES - MLA_ROPE)], axis=1).astype(BF16)
    km_ref[:, 0:LANES] = ckv.astype(BF16)
    km_ref[:, LANES:2 * LANES] = kpe_full.astype(BF16)
    vtm_ref[0, 0:MLA_KV_RANK, :] = ckv.T.astype(BF16)
    vtm_ref[0, MLA_KV_RANK:MLA_KV_RANK + BF16_ROWS, :] = _ones_then_zero_rows(tm)
    c_diff = DIFF_SCALE * LOG2E
    dvt = dv.T
    for g in range(DIFF_KV_HEADS):
        for r in range(DIFF_REP):
            for mm in range(2):
                lo = ((g * DIFF_REP + r) * 2 + mm) * DIFF_D
                parts = [zeros(mm * DIFF_D)] if mm else []
                parts += [dq[:, lo:lo + DIFF_D] * c_diff, zeros(LANES - (mm + 1) * DIFF_D)]
                qd_ref[0, g, 0, mm * DIFF_REP + r] = jnp.concatenate(parts, axis=1).astype(BF16)
        kd_ref[0, g, 0] = jnp.concatenate([dk[:, g * 2 * DIFF_D:(g + 1) * 2 * DIFF_D], zeros(LANES - 2 * DIFF_D)],
                                          axis=1).astype(BF16)
        vtd_ref[0, g, 0, 0:DIFF_V, :] = dvt[g * DIFF_V:(g + 1) * DIFF_V, :].astype(BF16)
        vtd_ref[0, g, 0, DIFF_V:DIFF_V + BF16_ROWS, :] = _ones_then_zero_rows(tm)


def _in_proj(x2d, n1, w_in_p, cos, sin, qn, wuqn, wuqr, wukt, kvn, tm, seq=None):
    m = x2d.shape[0]
    tb = cos.shape[0] // tm
    row = lambda i: (i, 0)
    fix = lambda i: (0, 0)
    fix3 = lambda i: (0, 0, 0)
    prompt = seq is not None
    if prompt:
        nb, ts = m // seq, seq // tm
        hq, hk = FLASH_TQ // tm, FLASH_TK // tm
        nq, nk = seq // FLASH_TQ, seq // FLASH_TK
        ea_m, ea_d = MLA_KV_RANK + BF16_ROWS, DIFF_V + BF16_ROWS
        f32_w = (256, 256, 128, None, 256, 256, 128, 32, 128, 128)
        out_specs = [pl.BlockSpec((tm, w), row) if w else
                     pl.BlockSpec((1, RET_HEADS * RET_DK, tm), lambda i: (i // ts, 0, i % ts)) for w in f32_w]
        out_shape = [jax.ShapeDtypeStruct((m, w), F32) if w else
                     jax.ShapeDtypeStruct((nb, RET_HEADS * RET_DK, seq), F32) for w in f32_w]
        out_specs += [
            pl.BlockSpec((1, MLA_HEADS, tm, 2 * LANES), lambda i: (i // hq, 0, i % hq, 0)),
            pl.BlockSpec((tm, 2 * LANES), row),
            pl.BlockSpec((1, ea_m, tm), lambda i: (i // hk, 0, i % hk)),
            pl.BlockSpec((1, DIFF_KV_HEADS, 1, 2 * DIFF_REP, tm, LANES),
                         lambda i: (i // ts, 0, (i % ts) // hq, 0, i % hq, 0)),
            pl.BlockSpec((1, DIFF_KV_HEADS, 1, tm, LANES), lambda i: (i // ts, 0, (i % ts) // hk, i % hk, 0)),
            pl.BlockSpec((1, DIFF_KV_HEADS, 1, ea_d, tm), lambda i: (i // ts, 0, (i % ts) // hk, 0, i % hk)),
        ]
        out_shape += [
            jax.ShapeDtypeStruct((nb * nq, MLA_HEADS, FLASH_TQ, 2 * LANES), BF16),
            jax.ShapeDtypeStruct((m, 2 * LANES), BF16),
            jax.ShapeDtypeStruct((nb * nk, ea_m, FLASH_TK), BF16),
            jax.ShapeDtypeStruct((nb, DIFF_KV_HEADS, nq, 2 * DIFF_REP, FLASH_TQ, LANES), BF16),
            jax.ShapeDtypeStruct((nb, DIFF_KV_HEADS, nk, FLASH_TK, LANES), BF16),
            jax.ShapeDtypeStruct((nb, DIFF_KV_HEADS, nk, ea_d, FLASH_TK), BF16),
        ]
    else:
        out_w = (256, 256, 128, 128, 256, 256, 512, 128, 128, 32, 256, 128, 128)
        out_specs = [pl.BlockSpec((tm, w), row) for w in out_w]
        out_shape = [jax.ShapeDtypeStruct((m, w), F32) for w in out_w]
    return pl.pallas_call(
        functools.partial(_in_proj_body, prompt=prompt),
        grid=(m // tm,),
        in_specs=[
            pl.BlockSpec((tm, D_MODEL), row),
            pl.BlockSpec((1, D_MODEL), fix),
            pl.BlockSpec((2, D_MODEL, _IN_WIDTH), fix3, pipeline_mode=pl.Buffered(1)),
            pl.BlockSpec((tm, LANES), lambda i: (i % tb, 0)),
            pl.BlockSpec((tm, LANES), lambda i: (i % tb, 0)),
            pl.BlockSpec((1, 256), fix),
            pl.BlockSpec((2, 256, 256), fix3),
            pl.BlockSpec((2, 256, 128), fix3),
            pl.BlockSpec((2, MLA_HEADS, MLA_NOPE, MLA_KV_RANK), lambda i: (0, 0, 0, 0)),
            pl.BlockSpec((1, MLA_KV_RANK), fix),
        ],
        out_specs=out_specs,
        out_shape=out_shape,
        compiler_params=_cparams(1),
        name="in_proj",
    )(x2d, n1, w_in_p, cos, sin, qn, wuqn, wuqr, wukt, kvn)


def _softplus(y):
    return jnp.maximum(y, 0.0) + jnp.log1p(jnp.exp(-jnp.abs(y)))


def _lru_gates(xc, wa_ref, ba_ref, wx_ref, bx_ref, lam_ref):
    r = jax.nn.sigmoid(_dot3(xc, wa_ref[0], wa_ref[1]) + ba_ref[...])
    i = jax.nn.sigmoid(_dot3(xc, wx_ref[0], wx_ref[1]) + bx_ref[...])
    log_a = -LRU_C * r * _softplus(-lam_ref[...])
    a = jnp.exp(log_a)
    th = jnp.tanh(log_a)
    b = jnp.sqrt(-2.0 * th / (1.0 - th)) * (i * xc)
    return a, b


def _lru_seq_body(ax_ref, ag_ref, buf0_ref, h0_ref, cw_ref, cb_ref, wa_ref, ba_ref, wx_ref, bx_ref, lam_ref,
                  y_ref, nbuf_ref, hl_ref, xbuf, hcar):
    t = pl.program_id(1)
    tt = ax_ref.shape[1]

    @pl.when(t == 0)
    def _():
        xbuf[0:8, :] = jnp.zeros((8, LRU_W), F32)
        xbuf[5:8, :] = buf0_ref[0]
        hcar[...] = h0_ref[0]

    xbuf[8:8 + tt, :] = ax_ref[0]
    cw = cw_ref[...]
    xc = cb_ref[...] + sum(xbuf[5 + k:5 + k + tt, :] * cw[k:k + 1, :] for k in range(CONV_W))
    a, b = _lru_gates(xc, wa_ref, ba_ref, wx_ref, bx_ref, lam_ref)
    rowi = lax.broadcasted_iota(jnp.int32, (tt, LRU_W), 0)
    d = 1
    while d < tt:
        keep = rowi >= d
        a_sh = jnp.where(keep, pltpu.roll(a, d, 0), 1.0)
        b_sh = jnp.where(keep, pltpu.roll(b, d, 0), 0.0)
        b = a * b_sh + b
        a = a * a_sh
        d *= 2
    hs = b + a * hcar[...]
    y_ref[0] = hs * jax.nn.gelu(ag_ref[0])
    hcar[...] = hs[tt - 1:tt, :]
    xbuf[0:8, :] = xbuf[tt:tt + 8, :]

    @pl.when(t == pl.num_programs(1) - 1)
    def _():
        nbuf_ref[0] = xbuf[5:8, :]
        hl_ref[0] = hs[tt - 1:tt, :]


def _lru_seq(ax, ag, buf0, h0, cw, cb, wa, ba, wx, bx, lam):
    b, s, _ = ax.shape
    tt = LRU_TT
    seq = lambda i, t: (i, t, 0)
    per_b = lambda i, t: (i, 0, 0)
    fix = lambda i, t: (0, 0)
    return pl.pallas_call(
        _lru_seq_body,
        grid=(b, s // tt),
        in_specs=[
            pl.BlockSpec((1, tt, LRU_W), seq), pl.BlockSpec((1, tt, LRU_W), seq),
            pl.BlockSpec((1, CONV_W - 1, LRU_W), per_b), pl.BlockSpec((1, 1, LRU_W), per_b),
            pl.BlockSpec((CONV_W, LRU_W), fix), pl.BlockSpec((1, LRU_W), fix),
            pl.BlockSpec((2, LRU_W, LRU_W), lambda i, t: (0, 0, 0)), pl.BlockSpec((1, LRU_W), fix),
            pl.BlockSpec((2, LRU_W, LRU_W), lambda i, t: (0, 0, 0)), pl.BlockSpec((1, LRU_W), fix),
            pl.BlockSpec((1, LRU_W), fix),
        ],
        out_specs=[pl.BlockSpec((1, tt, LRU_W), seq), pl.BlockSpec((1, CONV_W - 1, LRU_W), per_b),
                   pl.BlockSpec((1, 1, LRU_W), per_b)],
        out_shape=[jax.ShapeDtypeStruct((b, s, LRU_W), F32), jax.ShapeDtypeStruct((b, CONV_W - 1, LRU_W), F32),
                   jax.ShapeDtypeStruct((b, 1, LRU_W), F32)],
        scratch_shapes=[pltpu.VMEM((tt + 8, LRU_W), F32), pltpu.VMEM((1, LRU_W), F32)],
        compiler_params=_cparams(2),
        name="lru_seq",
    )(ax, ag, buf0, h0.reshape(b, 1, LRU_W), cw, cb, wa, ba, wx, bx, lam)


def _lru_step_body(ax_ref, ag_ref, b0_ref, b1_ref, b2_ref, h0_ref, cw_ref, cb_ref, wa_ref, ba_ref, wx_ref, bx_ref,
                   lam_ref, y_ref, hn_ref):
    cw = cw_ref[...]
    xc = (cb_ref[...] + b0_ref[...] * cw[0:1, :] + b1_ref[...] * cw[1:2, :] + b2_ref[...] * cw[2:3, :]
          + ax_ref[...] * cw[3:4, :])
    a, b = _lru_gates(xc, wa_ref, ba_ref, wx_ref, bx_ref, lam_ref)
    hs = b + a * h0_ref[...]
    y_ref[...] = hs * jax.nn.gelu(ag_ref[...])
    hn_ref[...] = hs


def _lru_step(ax, ag, buf, h0, cw, cb, wa, ba, wx, bx, lam):
    n = ax.shape[0]
    return pl.pallas_call(
        _lru_step_body,
        out_shape=[jax.ShapeDtypeStruct((n, LRU_W), F32), jax.ShapeDtypeStruct((n, LRU_W), F32)],
        compiler_params=pltpu.CompilerParams(vmem_limit_bytes=VMEM_LIMIT_BYTES),
        name="lru_step",
    )(ax, ag, buf[:, 0], buf[:, 1], buf[:, 2], h0, cw, cb, wa, ba, wx, bx, lam)


def _group_norm_gate(o, gn, g):
    mu = jnp.mean(o, axis=-1, keepdims=True)
    var = jnp.mean(jnp.square(o - mu), axis=-1, keepdims=True)
    return jax.nn.silu(g) * ((o - mu) * lax.rsqrt(var + EPS) * gn)


def _ret_seq_body(q_ref, kt_ref, v_ref, g_ref, s0_ref, dmat_ref, rdec_ref, cdec_ref, gl_ref, gn_ref,
                  y_ref, sn_ref, state):
    c = pl.program_id(1)

    @pl.when(c == 0)
    def _():
        state[...] = s0_ref[0]

    q = q_ref[0]
    kt = kt_ref[0]
    v = v_ref[0]
    g = g_ref[0]
    for hh in range(RET_HEADS):
        qh = q[:, hh * RET_DK:(hh + 1) * RET_DK]
        kth = kt[hh * RET_DK:(hh + 1) * RET_DK, :]
        vh = _split(v[:, hh * RET_DV:(hh + 1) * RET_DV])
        s_old = state[hh]
        s = _dot3v(qh, kth) * dmat_ref[hh]
        o = _dot3(s, *vh) + _dot3v(qh, s_old) * rdec_ref[hh]
        state[hh] = gl_ref[hh] * s_old + _dot3(kth * cdec_ref[hh], *vh)
        sl = slice(hh * RET_DV, (hh + 1) * RET_DV)
        y_ref[0, :, sl] = _group_norm_gate(o, gn_ref[:, sl], g[:, sl])

    @pl.when(c == pl.num_programs(1) - 1)
    def _():
        sn_ref[0] = state[...]


def _ret_tables(chunk):
    logg = jnp.log1p(-jnp.exp2(-5.0 - jnp.arange(RET_HEADS, dtype=F32)))
    idx = jnp.arange(chunk, dtype=F32)
    diff = idx[:, None] - idx[None, :]
    dmat = jnp.where(diff >= 0, jnp.exp(jnp.maximum(diff, 0.0)[None] * logg[:, None, None]), 0.0)
    rdec = jnp.exp((idx + 1.0)[None, :, None] * logg[:, None, None])
    cdec = jnp.exp((chunk - 1.0 - idx)[None, None, :] * logg[:, None, None])
    gl = jnp.broadcast_to(jnp.exp(chunk * logg)[:, None, None], (RET_HEADS, RET_DK, RET_DV))
    return dmat, rdec, cdec, gl, logg


def _ret_seq(q, kt, v, g, s0, gn):
    b, s, _ = q.shape
    ch = RET_L
    dmat, rdec, cdec, gl, _ = _ret_tables(ch)
    seq = lambda i, c: (i, c, 0)
    fix3 = lambda i, c: (0, 0, 0)
    return pl.pallas_call(
        _ret_seq_body,
        grid=(b, s // ch),
        in_specs=[
            pl.BlockSpec((1, ch, RET_HEADS * RET_DK), seq),
            pl.BlockSpec((1, RET_HEADS * RET_DK, ch), lambda i, c: (i, 0, c)),
            pl.BlockSpec((1, ch, GROUP_W), seq), pl.BlockSpec((1, ch, GROUP_W), seq),
            pl.BlockSpec((1, RET_HEADS, RET_DK, RET_DV), lambda i, c: (i, 0, 0, 0)),
            pl.BlockSpec((RET_HEADS, ch, ch), fix3), pl.BlockSpec((RET_HEADS, ch, 1), fix3),
            pl.BlockSpec((RET_HEADS, 1, ch), fix3), pl.BlockSpec((RET_HEADS, RET_DK, RET_DV), fix3),
            pl.BlockSpec((1, GROUP_W), lambda i, c: (0, 0)),
        ],
        out_specs=[pl.BlockSpec((1, ch, GROUP_W), seq),
                   pl.BlockSpec((1, RET_HEADS, RET_DK, RET_DV), lambda i, c: (i, 0, 0, 0))],
        out_shape=[jax.ShapeDtypeStruct((b, s, GROUP_W), F32),
                   jax.ShapeDtypeStruct((b, RET_HEADS, RET_DK, RET_DV), F32)],
        scratch_shapes=[pltpu.VMEM((RET_HEADS, RET_DK, RET_DV), F32)],
        compiler_params=_cparams(2),
        name="ret_seq",
    )(q, kt, v, g, s0, dmat, rdec, cdec, gl, gn)


def _ret_step_body(q_ref, k_ref, v_ref, g_ref, s0_ref, dec_ref, gn_ref, y_ref, sn_ref):
    q = q_ref[...]
    k = k_ref[...]
    v = v_ref[...]
    s0 = s0_ref[...]
    dec = dec_ref[...]
    qk = jnp.sum(q * k, axis=1, keepdims=True)
    o = qk * v + jnp.sum(q * s0, axis=1, keepdims=True) * dec
    sn_ref[...] = dec * s0 + k * v
    y_ref[...] = _group_norm_gate(o, gn_ref[...], g_ref[...])


def _ret_step(q, k, v, g, s0, gn):
    n = q.shape[0]
    x = n * RET_HEADS
    xb = 64
    _, _, _, _, logg = _ret_tables(1)
    dec = jnp.broadcast_to(jnp.exp(logg)[None, :, None, None], (xb // RET_HEADS, RET_HEADS, 1, RET_DV))
    dec = dec.reshape(xb, 1, RET_DV)
    gn3 = jnp.broadcast_to(gn.reshape(1, RET_HEADS, 1, RET_DV), (xb // RET_HEADS, RET_HEADS, 1, RET_DV))
    gn3 = gn3.reshape(xb, 1, RET_DV)
    blk = lambda i: (i, 0, 0)
    fix = lambda i: (0, 0, 0)
    y, sn = pl.pallas_call(
        _ret_step_body,
        grid=(x // xb,),
        in_specs=[pl.BlockSpec((xb, RET_DK, 1), blk), pl.BlockSpec((xb, RET_DK, 1), blk),
                  pl.BlockSpec((xb, 1, RET_DV), blk), pl.BlockSpec((xb, 1, RET_DV), blk),
                  pl.BlockSpec((xb, RET_DK, RET_DV), blk),
                  pl.BlockSpec((xb, 1, RET_DV), fix), pl.BlockSpec((xb, 1, RET_DV), fix)],
        out_specs=[pl.BlockSpec((xb, 1, RET_DV), blk), pl.BlockSpec((xb, RET_DK, RET_DV), blk)],
        out_shape=[jax.ShapeDtypeStruct((x, 1, RET_DV), F32), jax.ShapeDtypeStruct((x, RET_DK, RET_DV), F32)],
        compiler_params=_cparams(1),
        name="ret_step",
    )(q.reshape(x, RET_DK, 1), k.reshape(x, RET_DK, 1), v.reshape(x, 1, RET_DV), g.reshape(x, 1, RET_DV),
      s0.reshape(x, RET_DK, RET_DV), dec, gn3)
    return y.reshape(n, GROUP_W), sn.reshape(n, RET_HEADS, RET_DK, RET_DV)


def _diff_lambda(lq1_ref, lk1_ref, lq2_ref, lk2_ref, lam_init):
    return (jnp.exp(jnp.sum(lq1_ref[...] * lk1_ref[...], axis=-1, keepdims=True))
            - jnp.exp(jnp.sum(lq2_ref[...] * lk2_ref[...], axis=-1, keepdims=True)) + lam_init)


def _flash_body(*refs, tq, tk, e, diff, lam_init):
    n_in = 8 if diff else 3
    q_ref, k_ref, vt_ref = refs[:3]
    o_ref = refs[n_in]
    scr = refs[n_in + 1:]
    n_blk = len(scr) // 2
    m_scrs, acc_scrs = scr[:n_blk], scr[n_blk:]
    if diff:
        lq1_ref, lk1_ref, lq2_ref, lk2_ref, sub_ref = refs[3:8]
    qi = pl.program_id(2)
    cols = q_ref.shape[3]
    cb_w = cols // n_blk
    n_full = (qi * tq) // tk
    for m_scr, acc_scr in zip(m_scrs, acc_scrs):
        m_scr[...] = jnp.full(m_scr.shape, -jnp.inf, F32)
        acc_scr[...] = jnp.zeros(acc_scr.shape, F32)

    def step(j, masked):
        kj = k_ref[0, 0, j]
        vj = vt_ref[0, 0, j]
        sts = [_dot_nt(kj, q_ref[0, 0, 0, cb * cb_w:(cb + 1) * cb_w, :]) for cb in range(n_blk)]
        for cb, (st, m_scr, acc_scr) in enumerate(zip(sts, m_scrs, acc_scrs)):
            if masked:
                kpos = j * tk + lax.broadcasted_iota(jnp.int32, st.shape, 0)
                qpos = qi * tq + ((cb * cb_w + lax.broadcasted_iota(jnp.int32, st.shape, 1)) & (tq - 1))
                st = jnp.where(kpos <= qpos, st, -jnp.inf)
            m_old = m_scr[...]
            m_new = jnp.maximum(m_old, jnp.max(st, axis=0, keepdims=True))
            p = jnp.exp2(st - m_new).astype(BF16)
            acc_scr[...] = jnp.exp2(m_old - m_new) * acc_scr[...] + _dot(vj, p)
            m_scr[...] = m_new

    def full_step(j, carry):
        step(j, False)
        return carry

    lax.fori_loop(0, n_full, full_step, 0)
    step(n_full, True)
    acc = jnp.concatenate([a[...] for a in acc_scrs], axis=1) if n_blk > 1 else acc_scrs[0][...]
    o = acc[0:e, :] / acc[e:e + 1, :]
    if diff:
        lam = _diff_lambda(lq1_ref, lk1_ref, lq2_ref, lk2_ref, lam_init)
        half = cols // 2
        a = o[:, 0:half] - lam * o[:, half:cols]
        ms = jnp.mean(a * a, axis=0, keepdims=True)
        y = a * lax.rsqrt(ms + EPS) * sub_ref[...] * (1.0 - lam_init)
        o_ref[...] = jnp.concatenate([y[:, r * tq:(r + 1) * tq] for r in range(half // tq)], axis=0).T
    else:
        for hh in range(cols // tq):
            o_ref[:, hh * e:(hh + 1) * e] = o[:, hh * tq:(hh + 1) * tq].T


def _flash(q, k, vt, extra=None, lam_init=0.0):
    b, hk, nq, cols, dk = q.shape
    nk, ea = k.shape[2], vt.shape[3]
    e = ea - BF16_ROWS
    diff = extra is not None
    out_w = (cols // 2 if diff else cols) // FLASH_TQ * e
    in_specs = [
        pl.BlockSpec((1, 1, 1, cols, dk), lambda i, h, j: (i, h, j, 0, 0)),
        pl.BlockSpec((1, 1, nk, FLASH_TK, dk), lambda i, h, j: (i, h, 0, 0, 0)),
        pl.BlockSpec((1, 1, nk, ea, FLASH_TK), lambda i, h, j: (i, h, 0, 0, 0)),
    ]
    args = [q, k, vt]
    if diff:
        in_specs += [pl.BlockSpec((1, DIFF_D), lambda i, h, j: (0, 0))] * 4
        in_specs += [pl.BlockSpec((DIFF_V, 1), lambda i, h, j: (0, 0))]
        args += list(extra)
    return pl.pallas_call(
        functools.partial(_flash_body, tq=FLASH_TQ, tk=FLASH_TK, e=e, diff=diff, lam_init=lam_init),
        grid=(b, hk, nq),
        in_specs=in_specs,
        out_specs=pl.BlockSpec((FLASH_TQ, out_w), lambda i, h, j: (i * nq + j, h)),
        out_shape=jax.ShapeDtypeStruct((b * nq * FLASH_TQ, hk * out_w), F32),
        scratch_shapes=([pltpu.VMEM((1, FLASH_CB), F32)] * (cols // FLASH_CB)
                        + [pltpu.VMEM((ea, FLASH_CB), F32)] * (cols // FLASH_CB)),
        compiler_params=_cparams(3),
        name="flash_diff" if diff else "flash_mla",
    )(*args)


def _page_copies(pt_ref, n, slot, srcs, bufs, sems, n_pages, pick):
    copies = []
    for j in range(n_pages):
        page = pt_ref[n, j]
        for src, buf, sem, row_major in zip(srcs, bufs, sems, (pick == "mla", False)):
            if row_major:
                dst = buf.at[slot, pl.ds(j * PAGE, PAGE), :]
            else:
                dst = buf.at[slot, :, pl.ds(j * PAGE, PAGE)]
            copies.append(pltpu.make_async_copy(src(page), dst, sem.at[slot]))
    return copies


def _mla_decode_body(pt_ref, ql_ref, qp_ref, cn_ref, kn_ref, ckv_hbm, kpet_hbm, o_ref, cbuf, kbuf, c16, sem_c, sem_k,
                     *, layer, n_pages):
    n = pl.program_id(0)
    slot = n % 2
    srcs = (lambda pg: ckv_hbm.at[layer, pg], lambda pg: kpet_hbm.at[layer, pg])

    def copies(nn, sl):
        return _page_copies(pt_ref, nn, sl, srcs, (cbuf, kbuf), (sem_c, sem_k), n_pages, "mla")

    @pl.when(n == 0)
    def _():
        for cp in copies(0, 0):
            cp.start()

    @pl.when(n + 1 < pl.num_programs(0))
    def _():
        for cp in copies(n + 1, 1 - slot):
            cp.start()

    for cp in copies(n, slot):
        cp.wait()

    ql = ql_ref[0]
    qp = qp_ref[0]
    cn = cn_ref[0]
    kn = kn_ref[0]
    qlb, qpb = ql.astype(BF16), qp.astype(BF16)
    s_own = (jnp.sum(ql * cn, axis=-1, keepdims=True) + jnp.sum(qp * kn, axis=-1, keepdims=True)) * MLA_SCALE
    n_chunks = (n_pages * PAGE) // DECODE_CHUNK
    scores = []
    for c in range(n_chunks):
        lo = c * DECODE_CHUNK
        c16[lo:lo + DECODE_CHUNK, :] = cbuf[slot, lo:lo + DECODE_CHUNK, :].astype(BF16)
        s = _dot_nt(qlb, c16[lo:lo + DECODE_CHUNK, :]) + _dot(qpb, kbuf[slot, :, lo:lo + DECODE_CHUNK].astype(BF16))
        scores.append(s * MLA_SCALE)
    m = s_own
    for s in scores:
        m = jnp.maximum(m, jnp.max(s, axis=-1, keepdims=True))
    p_own = jnp.exp(s_own - m)
    den = p_own
    acc = p_own * cn
    for c, s in enumerate(scores):
        lo = c * DECODE_CHUNK
        p = jnp.exp(s - m)
        den = den + jnp.sum(p, axis=-1, keepdims=True)
        acc = acc + _dot(p.astype(BF16), c16[lo:lo + DECODE_CHUNK, :])
    o_ref[0] = acc / den


def _mla_decode(page_table, q_lat, q_pe, ckv_new, kpe_new, cache_ckv, cache_kpet, layer):
    n, n_pages = page_table.shape
    keys = n_pages * PAGE
    blk = lambda i, pt: (i, 0, 0)
    grid_spec = pltpu.PrefetchScalarGridSpec(
        num_scalar_prefetch=1,
        grid=(n,),
        in_specs=[pl.BlockSpec((1, MLA_HEADS, MLA_KV_RANK), blk), pl.BlockSpec((1, MLA_HEADS, MLA_ROPE), blk),
                  pl.BlockSpec((1, 1, MLA_KV_RANK), blk), pl.BlockSpec((1, 1, MLA_ROPE), blk),
                  pl.BlockSpec(memory_space=pl.ANY), pl.BlockSpec(memory_space=pl.ANY)],
        out_specs=pl.BlockSpec((1, MLA_HEADS, MLA_KV_RANK), blk),
        scratch_shapes=[pltpu.VMEM((2, keys, MLA_KV_RANK), F32), pltpu.VMEM((2, MLA_ROPE, keys), F32),
                        pltpu.VMEM((keys, MLA_KV_RANK), BF16),
                        pltpu.SemaphoreType.DMA((2,)), pltpu.SemaphoreType.DMA((2,))],
    )
    return pl.pallas_call(
        functools.partial(_mla_decode_body, layer=layer, n_pages=n_pages),
        grid_spec=grid_spec,
        out_shape=jax.ShapeDtypeStruct((n, MLA_HEADS, MLA_KV_RANK), F32),
        compiler_params=_cparams(1),
        name="mla_decode",
    )(page_table, q_lat.reshape(n, MLA_HEADS, MLA_KV_RANK), q_pe.reshape(n, MLA_HEADS, MLA_ROPE),
      ckv_new.reshape(n, 1, MLA_KV_RANK), kpe_new.reshape(n, 1, MLA_ROPE), cache_ckv, cache_kpet)


def _diff_decode_body(pt_ref, q_ref, kn_ref, vn_ref, lq1_ref, lk1_ref, lq2_ref, lk2_ref, sub_ref, kt_hbm, vt_hbm,
                      o_ref, kbuf, vbuf, sem_k, sem_v, *, layer, n_pages, lam_init):
    n = pl.program_id(0)
    g = pl.program_id(1)
    step = n * DIFF_KV_HEADS + g
    slot = step % 2
    srcs_of = lambda gg: (lambda pg: kt_hbm.at[layer, pg, gg], lambda pg: vt_hbm.at[layer, pg, gg])

    def copies(nn, gg, sl):
        return _page_copies(pt_ref, nn, sl, srcs_of(gg), (kbuf, vbuf), (sem_k, sem_v), n_pages, "diff")

    @pl.when(step == 0)
    def _():
        for cp in copies(0, 0, 0):
            cp.start()

    @pl.when(step + 1 < pl.num_programs(0) * DIFF_KV_HEADS)
    def _():
        nxt = step + 1
        for cp in copies(nxt // DIFF_KV_HEADS, nxt % DIFF_KV_HEADS, 1 - slot):
            cp.start()

    for cp in copies(n, g, slot):
        cp.wait()

    q = q_ref[0, 0]
    kn = kn_ref[0, 0]
    vn = vn_ref[0, 0]
    qb = q.astype(BF16)
    s_own = jnp.sum(q * kn, axis=-1, keepdims=True) * DIFF_SCALE
    n_chunks = (n_pages * PAGE) // DECODE_CHUNK
    scores = []
    for c in range(n_chunks):
        lo = c * DECODE_CHUNK
        scores.append(_dot(qb, kbuf[slot, :, lo:lo + DECODE_CHUNK].astype(BF16)) * DIFF_SCALE)
    m = s_own
    for s in scores:
        m = jnp.maximum(m, jnp.max(s, axis=-1, keepdims=True))
    p_own = jnp.exp(s_own - m)
    den = p_own
    acc = p_own * vn
    for c, s in enumerate(scores):
        lo = c * DECODE_CHUNK
        p = jnp.exp(s - m)
        den = den + jnp.sum(p, axis=-1, keepdims=True)
        acc = acc + _dot_nt(p.astype(BF16), vbuf[slot, :, lo:lo + DECODE_CHUNK].astype(BF16))
    o = acc / den
    lam = _diff_lambda(lq1_ref, lk1_ref, lq2_ref, lk2_ref, lam_init)
    for r in range(DIFF_REP):
        a = o[2 * r:2 * r + 1, :] - lam * o[2 * r + 1:2 * r + 2, :]
        o_ref[0, 0, r:r + 1, :] = _rms(a, sub_ref[...]) * (1.0 - lam_init)


def _diff_decode(page_table, qbd, k_new, v_new, lam_refs, subln, cache_kt, cache_vt, layer, lam_init):
    n, n_pages = page_table.shape
    keys = n_pages * PAGE
    blk = lambda i, g, pt: (i, g, 0, 0)
    fix = lambda i, g, pt: (0, 0)
    grid_spec = pltpu.PrefetchScalarGridSpec(
        num_scalar_prefetch=1,
        grid=(n, DIFF_KV_HEADS),
        in_specs=[pl.BlockSpec((1, 1, 2 * DIFF_REP, 2 * DIFF_D), blk), pl.BlockSpec((1, 1, 1, 2 * DIFF_D), blk),
                  pl.BlockSpec((1, 1, 1, DIFF_V), blk)]
        + [pl.BlockSpec((1, DIFF_D), fix)] * 4 + [pl.BlockSpec((1, DIFF_V), fix)]
        + [pl.BlockSpec(memory_space=pl.ANY), pl.BlockSpec(memory_space=pl.ANY)],
        out_specs=pl.BlockSpec((1, 1, DIFF_REP, DIFF_V), blk),
        scratch_shapes=[pltpu.VMEM((2, 2 * DIFF_D, keys), F32), pltpu.VMEM((2, DIFF_V, keys), F32),
                        pltpu.SemaphoreType.DMA((2,)), pltpu.SemaphoreType.DMA((2,))],
    )
    return pl.pallas_call(
        functools.partial(_diff_decode_body, layer=layer, n_pages=n_pages, lam_init=lam_init),
        grid_spec=grid_spec,
        out_shape=jax.ShapeDtypeStruct((n, DIFF_KV_HEADS, DIFF_REP, DIFF_V), F32),
        compiler_params=_cparams(2),
        name="diff_decode",
    )(page_table, qbd, k_new, v_new, *lam_refs, subln, cache_kt, cache_vt)


def _post_body(x_ref, ya_ref, yb_ref, ol_ref, yd_ref, wuv_ref, wout_ref, n2_ref, wr_ref, br_ref,
               x1_ref, hm_ref, cmb_ref):
    ol = ol_ref[...]
    yc = jnp.concatenate(
        [_dot3(ol[:, hh * MLA_KV_RANK:(hh + 1) * MLA_KV_RANK], wuv_ref[0, hh], wuv_ref[1, hh])
         for hh in range(MLA_HEADS)], axis=1)
    y = None
    for gi, yg in enumerate((ya_ref[...], yb_ref[...], yc, yd_ref[...])):
        rows = slice(gi * GROUP_W, (gi + 1) * GROUP_W)
        part = _dot3(yg, wout_ref[0, rows, :], wout_ref[1, rows, :])
        y = part if y is None else y + part
    x1 = x_ref[...] + y
    x1_ref[...] = x1
    hm = _rms(x1, n2_ref[...])
    hm_ref[...] = hm.astype(hm_ref.dtype)
    logits = _dot3(hm, wr_ref[0], wr_ref[1]) + br_ref[...]
    lane_i = lax.broadcasted_iota(jnp.int32, logits.shape, 1)
    lane = lane_i.astype(F32)
    far = float(LANES)
    gl = jnp.where((lane_i >= MOE_EXPERTS) & (lane_i < MOE_EXPERTS + MOE_GROUPS), logits, -jnp.inf)
    gmax = jnp.max(gl, axis=1, keepdims=True)
    gsel = jnp.min(jnp.where(gl == gmax, lane, far), axis=1, keepdims=True) - float(MOE_EXPERTS)
    gw = 1.0 / jnp.sum(jnp.exp(gl - gmax), axis=1, keepdims=True)
    lane_group = (lane_i >> 2).astype(F32)
    el = jnp.where((lane_i < MOE_EXPERTS) & (lane_group == gsel), logits, -jnp.inf)
    v1 = jnp.max(el, axis=1, keepdims=True)
    i1 = jnp.min(jnp.where(el == v1, lane, far), axis=1, keepdims=True)
    el2 = jnp.where(lane == i1, -jnp.inf, el)
    v2 = jnp.max(el2, axis=1, keepdims=True)
    i2 = jnp.min(jnp.where(el2 == v2, lane, far), axis=1, keepdims=True)
    e21 = jnp.exp(v2 - v1)
    w1 = 1.0 / (1.0 + e21)
    w2 = e21 / (1.0 + e21)
    cmb_ref[...] = jnp.where(lane == i1, w1 * gw, jnp.where(lane == i2, w2 * gw, 0.0))


def _post(x2d, ya, yb, ol, yd, wuv, wout, n2, wr, br, tm, hm_dtype):
    m = x2d.shape[0]
    row = lambda i: (i, 0)
    fix = lambda i: (0, 0)
    return pl.pallas_call(
        _post_body,
        grid=(m // tm,),
        in_specs=[pl.BlockSpec((tm, D_MODEL), row), pl.BlockSpec((tm, GROUP_W), row), pl.BlockSpec((tm, GROUP_W), row),
                  pl.BlockSpec((tm, MLA_HEADS * MLA_KV_RANK), row), pl.BlockSpec((tm, GROUP_W), row),
                  pl.BlockSpec((2, MLA_HEADS, MLA_KV_RANK, MLA_V), lambda i: (0, 0, 0, 0)),
                  pl.BlockSpec((2, D_MODEL, D_MODEL), lambda i: (0, 0, 0)), pl.BlockSpec((1, D_MODEL), fix),
                  pl.BlockSpec((2, D_MODEL, LANES), lambda i: (0, 0, 0)), pl.BlockSpec((1, LANES), fix)],
        out_specs=[pl.BlockSpec((tm, D_MODEL), row), pl.BlockSpec((tm, D_MODEL), row), pl.BlockSpec((tm, LANES), row)],
        out_shape=[jax.ShapeDtypeStruct((m, D_MODEL), F32), jax.ShapeDtypeStruct((m, D_MODEL), hm_dtype),
                   jax.ShapeDtypeStruct((m, LANES), F32)],
        compiler_params=_cparams(1),
        name="post",
    )(x2d, ya, yb, ol, yd, wuv, wout, n2, wr, br)


def _moe_body(hm_ref, cmb_ref, x1_ref, wg_ref, wu_ref, wd_ref, fn_ref, o_ref, *, final, precise):
    e = pl.program_id(1)

    @pl.when(e == 0)
    def _():
        o_ref[...] = x1_ref[...]

    if precise:
        mm = lambda a, w_ref: _dot3v(a, w_ref[0])
    else:
        mm = lambda a, w_ref: _dot(a.astype(BF16), w_ref[0].astype(BF16))
    h = hm_ref[...]
    hg = mm(h, wg_ref)
    hu = mm(h, wu_ref)
    cmb = cmb_ref[...]
    lane = lax.broadcasted_iota(jnp.int32, cmb.shape, 1)
    c = jnp.sum(jnp.where(lane == e, cmb, 0.0), axis=1, keepdims=True)
    act = jax.nn.silu(hg) * hu * c
    o_ref[...] += mm(act, wd_ref)
    if final:
        @pl.when(e == pl.num_programs(1) - 1)
        def _():
            o_ref[...] = _rms(o_ref[...], fn_ref[...])


def _moe(hm, cmb, x1, wg, wu, wd, fn, tm, final, precise):
    m = hm.shape[0]
    row = lambda i, e: (i, 0)
    wsel = lambda i, e: (e, 0, 0)
    return pl.pallas_call(
        functools.partial(_moe_body, final=final, precise=precise),
        grid=(m // tm, MOE_EXPERTS),
        in_specs=[pl.BlockSpec((tm, D_MODEL), row), pl.BlockSpec((tm, LANES), row), pl.BlockSpec((tm, D_MODEL), row),
                  pl.BlockSpec((1, D_MODEL, MOE_FF), wsel), pl.BlockSpec((1, D_MODEL, MOE_FF), wsel),
                  pl.BlockSpec((1, MOE_FF, D_MODEL), wsel), pl.BlockSpec((1, D_MODEL), lambda i, e: (0, 0))],
        out_specs=pl.BlockSpec((tm, D_MODEL), row),
        out_shape=jax.ShapeDtypeStruct((m, D_MODEL), F32),
        compiler_params=_cparams(2),
        name="moe",
    )(hm, cmb, x1, wg, wu, wd, fn)


def _rope_tables(pos):
    half = MLA_ROPE // 2
    inv = ROPE_THETA ** (-jnp.arange(half, dtype=F32) / half)
    ang = pos.astype(F32)[:, None] * inv[None, :]
    c, s = jnp.cos(ang), jnp.sin(ang)
    return (jnp.tile(jnp.concatenate([c, c], axis=-1), (1, 4)),
            jnp.tile(jnp.concatenate([-s, s], axis=-1), (1, 4)))


def _pad_cols(w, sizes, pads):
    pieces, off = [], 0
    for sz, pd in zip(sizes, pads):
        piece = w[:, off:off + sz]
        if pd > sz:
            piece = jnp.pad(piece, ((0, 0), (0, pd - sz)))
        pieces.append(piece)
        off += sz
    return jnp.concatenate(pieces, axis=1)


def _block_diag(w):
    nb, bw, _ = w.shape
    out = jnp.zeros((nb * bw, nb * bw), w.dtype)
    for i in range(nb):
        out = out.at[i * bw:(i + 1) * bw, i * bw:(i + 1) * bw].set(w[i])
    return out


def _hilo(w):
    hi = w.astype(BF16)
    return jnp.stack([hi, (w - hi.astype(F32)).astype(BF16)])


def _layer_weights(l, P):
    w = {}
    w["n1"] = P["norm1"][l][None, :]
    w["w_in"] = _hilo(_pad_cols(P["w_in"][l], _IN_SIZES, _IN_PAD))
    w["cw"] = P["conv_w"][l]
    w["cb"] = P["conv_b"][l][None, :]
    w["wa"] = _hilo(_block_diag(P["lru_wa"][l]))
    w["ba"] = P["lru_ba"][l][None, :]
    w["wx"] = _hilo(_block_diag(P["lru_wx"][l]))
    w["bx"] = P["lru_bx"][l][None, :]
    w["lam"] = P["lru_lambda"][l][None, :]
    w["gn"] = P["ret_gn"][l][None, :]
    w["qn"] = jnp.pad(P["mla_q_norm"][l], (0, 256 - MLA_Q_RANK))[None, :]
    wuq = P["mla_w_uq"][l].reshape(MLA_Q_RANK, MLA_HEADS, MLA_NOPE + MLA_ROPE)
    pad_rows = ((0, 256 - MLA_Q_RANK), (0, 0))
    w["wuqn"] = _hilo(jnp.pad(wuq[:, :, :MLA_NOPE].reshape(MLA_Q_RANK, MLA_HEADS * MLA_NOPE), pad_rows))
    w["wuqr"] = _hilo(jnp.pad(wuq[:, :, MLA_NOPE:].reshape(MLA_Q_RANK, MLA_HEADS * MLA_ROPE), pad_rows))
    w["wukt"] = _hilo(jnp.transpose(P["mla_w_uk"][l], (1, 2, 0)))
    w["kvn"] = P["mla_kv_norm"][l][None, :]
    w["wuv"] = _hilo(jnp.transpose(P["mla_w_uv"][l], (1, 0, 2)))
    w["lam_refs"] = tuple(P[k][l][None, :] for k in ("diff_lq1", "diff_lk1", "diff_lq2", "diff_lk2"))
    w["subln"] = P["diff_subln"][l]
    w["w_out"] = _hilo(P["w_out"][l])
    w["n2"] = P["norm2"][l][None, :]
    wr = jnp.concatenate([P["moe_w_re"][l], P["moe_w_rg"][l]], axis=1)
    w["wr"] = _hilo(jnp.pad(wr, ((0, 0), (0, LANES - wr.shape[1]))))
    br = jnp.concatenate([P["moe_b_re"][l], P["moe_b_rg"][l]])
    w["br"] = jnp.pad(br, (0, LANES - br.shape[0]))[None, :]
    w["moe"] = (P["moe_w_gate"][l], P["moe_w_up"][l], P["moe_w_down"][l])
    return w


def _prompt_mixers(l, w, pieces, b, s, lam_init):
    (lrux, lrug, rq, rkt, rv, rg, ckv, kpe, dk, dv, qm, km, vtm, qd, kd, vtd) = pieces
    tq, tk = FLASH_TQ, FLASH_TK
    nq, nk = s // tq, s // tk
    r3 = lambda a: a.reshape(b, s, a.shape[-1])
    zeros = lambda *sh: jnp.zeros(sh, F32)
    ya, nbuf, hlast = _lru_seq(r3(lrux), r3(lrug), zeros(b, CONV_W - 1, LRU_W), zeros(b, LRU_W), w["cw"], w["cb"],
                               w["wa"], w["ba"], w["wx"], w["bx"], w["lam"])
    yb, s_new = _ret_seq(r3(rq), rkt, r3(rv), r3(rg), zeros(b, RET_HEADS, RET_DK, RET_DV), w["gn"])
    ol = _flash(qm.reshape(b, 1, nq, MLA_HEADS * tq, qm.shape[-1]), km.reshape(b, 1, nk, tk, km.shape[-1]),
                vtm.reshape(b, 1, nk, vtm.shape[1], tk))
    yd = _flash(qd.reshape(b, DIFF_KV_HEADS, nq, 2 * DIFF_REP * tq, qd.shape[-1]), kd, vtd,
                extra=w["lam_refs"] + (w["subln"][:, None],), lam_init=lam_init)
    new = (nbuf, hlast.reshape(b, LRU_W), s_new, r3(ckv), r3(kpe),
           dk.reshape(b, s, DIFF_KV_HEADS, 2 * DIFF_D), dv.reshape(b, s, DIFF_KV_HEADS, DIFF_V))
    return ya.reshape(b * s, LRU_W), yb.reshape(b * s, GROUP_W), ol, yd, new


def _sample_mixers(l, w, pieces, n, lru_buf, lru_h, ret_s, caches, page_table, lam_init):
    (lrux, lrug, rq, rk, rv, rg, qlat, qpe, ckv, kpe, dq, dk, dv) = pieces
    cache_ckv, cache_kpet, cache_kt, cache_vt = caches
    ya, h_new = _lru_step(lrux, lrug, lru_buf, lru_h, w["cw"], w["cb"], w["wa"], w["ba"], w["wx"], w["bx"], w["lam"])
    nbuf = jnp.concatenate([lru_buf[:, 1:], lrux[:, None, :]], axis=1)
    yb, s_new = _ret_step(rq, rk, rv, rg, ret_s, w["gn"])
    ol = _mla_decode(page_table, qlat, qpe, ckv, kpe, cache_ckv, cache_kpet, l).reshape(n, MLA_HEADS * MLA_KV_RANK)
    q5 = dq.reshape(n, DIFF_KV_HEADS, DIFF_REP, 2, DIFF_D)
    zq = jnp.zeros_like(q5[..., 0, :])
    qbd = jnp.stack([jnp.concatenate([q5[..., 0, :], zq], -1), jnp.concatenate([zq, q5[..., 1, :]], -1)], axis=3)
    qbd = qbd.reshape(n, DIFF_KV_HEADS, 2 * DIFF_REP, 2 * DIFF_D)
    yd = _diff_decode(page_table, qbd, dk.reshape(n, DIFF_KV_HEADS, 1, 2 * DIFF_D),
                      dv.reshape(n, DIFF_KV_HEADS, 1, DIFF_V), w["lam_refs"], w["subln"][None, :],
                      cache_kt, cache_vt, l, lam_init).reshape(n, GROUP_W)
    new = (nbuf, h_new, s_new, ckv.reshape(n, 1, MLA_KV_RANK), kpe.reshape(n, 1, MLA_ROPE),
           dk.reshape(n, 1, DIFF_KV_HEADS, 2 * DIFF_D), dv.reshape(n, 1, DIFF_KV_HEADS, DIFF_V))
    return ya, yb, ol, yd, new


def kernel(x_prompt, x_sample, state_lru_conv, state_lru_h, state_ret, cache_mla_ckv, cache_mla_kpe, cache_diff_k, cache_diff_v, page_table, norm1, w_in, conv_w, conv_b, lru_wa, lru_ba, lru_wx, lru_bx, lru_lambda, ret_gn, mla_q_norm, mla_w_uq, mla_kv_norm, mla_w_uk, mla_w_uv, diff_lq1, diff_lk1, diff_lq2, diff_lk2, diff_subln, w_out, norm2, moe_w_rg, moe_b_rg, moe_w_re, moe_b_re, moe_w_gate, moe_w_up, moe_w_down, final_norm):
    P = dict(norm1=norm1, w_in=w_in, conv_w=conv_w, conv_b=conv_b, lru_wa=lru_wa, lru_ba=lru_ba,
             lru_wx=lru_wx, lru_bx=lru_bx, lru_lambda=lru_lambda, ret_gn=ret_gn, mla_q_norm=mla_q_norm,
             mla_w_uq=mla_w_uq, mla_kv_norm=mla_kv_norm, mla_w_uk=mla_w_uk, mla_w_uv=mla_w_uv,
             diff_lq1=diff_lq1, diff_lk1=diff_lk1, diff_lq2=diff_lq2, diff_lk2=diff_lk2,
             diff_subln=diff_subln, w_out=w_out, norm2=norm2, moe_w_rg=moe_w_rg, moe_b_rg=moe_b_rg,
             moe_w_re=moe_w_re, moe_b_re=moe_b_re, moe_w_gate=moe_w_gate, moe_w_up=moe_w_up,
             moe_w_down=moe_w_down)
    depth = norm1.shape[0]
    b, s, _ = x_prompt.shape
    n = x_sample.shape[0]
    past_len = page_table.shape[1] * PAGE
    cos_p, sin_p = _rope_tables(jnp.arange(s, dtype=jnp.int32))
    cos_s, sin_s = _rope_tables(jnp.full((1,), past_len, jnp.int32))
    cos_s, sin_s = jnp.broadcast_to(cos_s, (n, LANES)), jnp.broadcast_to(sin_s, (n, LANES))
    caches = (cache_mla_ckv, jnp.swapaxes(cache_mla_kpe, 2, 3),
              jnp.transpose(cache_diff_k, (0, 1, 3, 4, 2)), jnp.transpose(cache_diff_v, (0, 1, 3, 4, 2)))
    fn = final_norm[None, :]
    tm_p = 512
    tm_in = 512
    tm_moe = 1024
    xp = x_prompt.reshape(b * s, D_MODEL)
    xs = x_sample.reshape(n, D_MODEL)
    p_new, s_new = [], []
    for l in range(depth):
        w = _layer_weights(l, P)
        lam_init = 0.8 - 0.6 * math.exp(-0.3 * l)
        final = l == depth - 1
        proj = lambda x2d, cos, sin, tm, seq=None: _in_proj(x2d, w["n1"], w["w_in"], cos, sin, w["qn"], w["wuqn"],
                                                            w["wuqr"], w["wukt"], w["kvn"], tm, seq)
        ya, yb, ol, yd, st_p = _prompt_mixers(l, w, proj(xp, cos_p, sin_p, tm_in, s), b, s, lam_init)
        x1, hm, cmb = _post(xp, ya, yb, ol, yd, w["wuv"], w["w_out"], w["n2"], w["wr"], w["br"], tm_p, BF16)
        xp = _moe(hm, cmb, x1, *w["moe"], fn, tm_moe, final, False)
        ya, yb, ol, yd, st_s = _sample_mixers(l, w, proj(xs, cos_s, sin_s, n), n, state_lru_conv[l], state_lru_h[l],
                                              state_ret[l], caches, page_table, lam_init)
        x1, hm, cmb = _post(xs, ya, yb, ol, yd, w["wuv"], w["w_out"], w["n2"], w["wr"], w["br"], n, F32)
        xs = _moe(hm, cmb, x1, *w["moe"], fn, n, final, True)
        p_new.append(st_p)
        s_new.append(st_s)
    outs_p = [jnp.stack([st[i] for st in p_new]) for i in range(7)]
    outs_s = [jnp.stack([st[i] for st in s_new]) for i in range(7)]
    return (xp.reshape(b, s, D_MODEL), xs.reshape(n, 1, D_MODEL), *outs_p, *outs_s)
```
